```python
import jax, jax.numpy as jnp
from jax import lax
import numpy as np

D_MODEL = 1024
BATCH = 2
SEQ = 8192
DEPTH = 4
DEC_BATCH = 32
DEC_SEQ = 8
PAST_LEN = 8192
PAGE_SIZE = 128

N_HEADS = 8
HEAD_DIM = 64
ATT_WIDTH = N_HEADS * HEAD_DIM
CONV_CH = D_MODEL // 2
CONV_WIDTH = 31
MIX_WIDTH = ATT_WIDTH + CONV_CH
IN_WIDTH = 3 * ATT_WIDTH + 2 * CONV_CH + N_HEADS
D_FF = 2816
N_EXPERTS = 8
TOP_K = 2
D_FF_EXPERT = D_MODEL * 7 // 2
PLE_DIM = 256
Q_BLOCK = 128
N_DENSE = (DEPTH + 1) // 2
N_MOE = DEPTH // 2
EPS = 1e-6
FORGET_BIAS_INIT = 3.0
ATT_SCALE = HEAD_DIM ** -0.5

kernel_name = "fox_conformer_hymba_moe_step"

F32 = jnp.float32


def rms_norm(x, g):
    x32 = x.astype(F32)
    y = x32 * lax.rsqrt(jnp.mean(x32 * x32, axis=-1, keepdims=True) + EPS)
    return (y * g.astype(F32)).astype(x.dtype)


def layer_norm(x, g, b):
    x32 = x.astype(F32)
    xc = x32 - jnp.mean(x32, axis=-1, keepdims=True)
    y = xc * lax.rsqrt(jnp.mean(xc * xc, axis=-1, keepdims=True) + EPS)
    return (y * g.astype(F32) + b.astype(F32)).astype(x.dtype)


def mixer_inputs(h, g_mix, w_in, b_f, g_q, g_k):
    lead = h.shape[:-1]
    z = rms_norm(h, g_mix) @ w_in
    q, k, v, ua, ug, fl = jnp.split(
        z,
        [ATT_WIDTH, 2 * ATT_WIDTH, 3 * ATT_WIDTH,
         3 * ATT_WIDTH + CONV_CH, 3 * ATT_WIDTH + 2 * CONV_CH],
        axis=-1)
    heads = lead + (N_HEADS, HEAD_DIM)
    q = rms_norm(q.reshape(heads), g_q)
    k = rms_norm(k.reshape(heads), g_k)
    v = v.reshape(heads)
    logf = jax.nn.log_sigmoid(fl.astype(F32) + b_f.astype(F32))
    u = ua * jax.nn.sigmoid(ug)
    return q, k, v, logf, u


def fox_attend(q, cq, qpos, k, v, ck):
    s = jnp.einsum('bqhd,bkhd->bhqk', q, k, preferred_element_type=F32) * ATT_SCALE
    s = s + cq[..., :, None] - ck[..., None, :]
    mask = jnp.arange(k.shape[1])[None, :] <= qpos[:, None]
    s = jnp.where(mask, s, -jnp.inf)
    p = jax.nn.softmax(s, axis=-1)
    return jnp.einsum('bhqk,bkhd->bqhd', p.astype(v.dtype), v)


def fox_prompt(q, k, v, logf):
    b, t = q.shape[0], q.shape[1]
    ck = jnp.cumsum(logf, axis=1).transpose(0, 2, 1)
    nb = t // Q_BLOCK
    qb = q.reshape(b, nb, Q_BLOCK, N_HEADS, HEAD_DIM).swapaxes(0, 1)
    cqb = ck.reshape(b, N_HEADS, nb, Q_BLOCK).transpose(2, 0, 1, 3)

    def one_block(args):
        qi, cqi, bi = args
        qpos = bi * Q_BLOCK + jnp.arange(Q_BLOCK)
        return fox_attend(qi, cqi, qpos, k, v, ck)

    o = lax.map(one_block, (qb, cqb, jnp.arange(nb)))
    return o.swapaxes(0, 1).reshape(b, t, ATT_WIDTH)


def fox_sample(q, k, v, logf, k_past, v_past, logf_past):
    b, s_new = q.shape[0], q.shape[1]
    past = k_past.shape[1]
    k_all = jnp.concatenate([k_past.astype(k.dtype), k], axis=1)
    v_all = jnp.concatenate([v_past.astype(v.dtype), v], axis=1)
    ck = jnp.cumsum(jnp.concatenate([logf_past.astype(F32), logf], axis=1), axis=1).transpose(0, 2, 1)
    cq = ck[:, :, past:]
    qpos = past + jnp.arange(s_new)
    o = fox_attend(q, cq, qpos, k_all, v_all, ck)
    return o.reshape(b, s_new, ATT_WIDTH)


def conv_module(u_pad, w_dw, b_dw, g_ln, b_ln):
    filt = w_dw.astype(u_pad.dtype)[:, None, :]
    y = lax.conv_general_dilated(u_pad, filt, window_strides=(1,), padding='VALID',
                                 dimension_numbers=('NWC', 'WIO', 'NWC'),
                                 feature_group_count=CONV_CH)
    y = y + b_dw.astype(y.dtype)
    return jax.nn.silu(layer_norm(y, g_ln, b_ln))


def mixer_output(att, cnv, g_att, g_cv, w_out):
    m = jnp.concatenate([rms_norm(att, g_att), rms_norm(cnv, g_cv)], axis=-1)
    return m @ w_out


def swiglu(x, wg, wu, wd):
    return (jax.nn.silu(x @ wg) * (x @ wu)) @ wd


def moe_swiglu(x, w_router, wg, wu, wd):
    logits = (x @ w_router).astype(F32)
    top_v, top_i = lax.top_k(logits, TOP_K)
    top_w = jax.nn.softmax(top_v, axis=-1)
    gates = jnp.sum(jax.nn.one_hot(top_i, N_EXPERTS, dtype=F32) * top_w[..., None], axis=-2).astype(x.dtype)
    out = jnp.zeros_like(x)
    for e in range(N_EXPERTS):
        out = out + gates[:, e:e + 1] * swiglu(x, wg[e], wu[e], wd[e])
    return out


def setup_inputs(seed: int = 0) -> dict:
    key = jax.random.key(seed)
    keys = jax.random.split(key, 48)
    counter = [0]

    def nrm(shape, scale):
        k = keys[counter[0]]
        counter[0] += 1
        return scale * jax.random.normal(k, shape, F32)

    def gain(shape):
        return 1.0 + nrm(shape, 0.02)

    n_pages = PAST_LEN // PAGE_SIZE
    n_used = DEC_BATCH * n_pages
    n_pool = n_used + max(1, n_used // 4)

    x_prompt = nrm((BATCH, SEQ, D_MODEL), 1.0)
    x_sample = nrm((DEC_BATCH, DEC_SEQ, D_MODEL), 1.0)
    cache_k = nrm((DEPTH, n_pool, PAGE_SIZE, N_HEADS, HEAD_DIM), 1.0)
    cache_v = nrm((DEPTH, n_pool, PAGE_SIZE, N_HEADS, HEAD_DIM), 1.0)
    cache_logf = jax.nn.log_sigmoid(FORGET_BIAS_INIT + nrm((DEPTH, n_pool, PAGE_SIZE, N_HEADS), 1.0))
    state_conv = nrm((DEPTH, DEC_BATCH, CONV_WIDTH - 1, CONV_CH), 0.5)
    perm_key = keys[counter[0]]
    counter[0] += 1
    page_table = jax.random.permutation(perm_key, n_pool)[:n_used].reshape(DEC_BATCH, n_pages).astype(jnp.int32)
    p_prompt = nrm((DEPTH, BATCH, SEQ, PLE_DIM), 1.0)
    p_sample = nrm((DEPTH, DEC_BATCH, DEC_SEQ, PLE_DIM), 1.0)

    return {
        'x_prompt': x_prompt,
        'x_sample': x_sample,
        'cache_k': cache_k,
        'cache_v': cache_v,
        'cache_logf': cache_logf,
        'state_conv': state_conv,
        'page_table': page_table,
        'p_prompt': p_prompt,
        'p_sample': p_sample,
        'g_mix': gain((DEPTH, D_MODEL)),
        'w_in': nrm((DEPTH, D_MODEL, IN_WIDTH), D_MODEL ** -0.5),
        'b_forget': FORGET_BIAS_INIT + nrm((DEPTH, N_HEADS), 0.1),
        'g_q': gain((DEPTH, HEAD_DIM)),
        'g_k': gain((DEPTH, HEAD_DIM)),
        'w_dw': nrm((DEPTH, CONV_WIDTH, CONV_CH), CONV_WIDTH ** -0.5),
        'b_dw': nrm((DEPTH, CONV_CH), 0.02),
        'g_conv_ln': gain((DEPTH, CONV_CH)),
        'b_conv_ln': nrm((DEPTH, CONV_CH), 0.02),
        'g_out_att': gain((DEPTH, ATT_WIDTH)),
        'g_out_conv': gain((DEPTH, CONV_CH)),
        'w_out': nrm((DEPTH, MIX_WIDTH, D_MODEL), MIX_WIDTH ** -0.5),
        'g_ffn': gain((DEPTH, D_MODEL)),
        'w_ff_gate': nrm((N_DENSE, D_MODEL, D_FF), D_MODEL ** -0.5),
        'w_ff_up': nrm((N_DENSE, D_MODEL, D_FF), D_MODEL ** -0.5),
        'w_ff_down': nrm((N_DENSE, D_FF, D_MODEL), D_FF ** -0.5),
        'w_router': nrm((N_MOE, D_MODEL, N_EXPERTS), D_MODEL ** -0.5),
        'w_ex_gate': nrm((N_MOE, N_EXPERTS, D_MODEL, D_FF_EXPERT), D_MODEL ** -0.5),
        'w_ex_up': nrm((N_MOE, N_EXPERTS, D_MODEL, D_FF_EXPERT), D_MODEL ** -0.5),
        'w_ex_down': nrm((N_MOE, N_EXPERTS, D_FF_EXPERT, D_MODEL), D_FF_EXPERT ** -0.5),
        'g_ple': gain((DEPTH, D_MODEL)),
        'w_ple_proj': nrm((DEPTH, PLE_DIM, D_MODEL), PLE_DIM ** -0.5),
        'w_ple_gate': nrm((DEPTH, D_MODEL, D_MODEL), D_MODEL ** -0.5),
    }


def reference(x_prompt, x_sample, cache_k, cache_v, cache_logf, state_conv, page_table,
              p_prompt, p_sample, g_mix, w_in, b_forget, g_q, g_k, w_dw, b_dw,
              g_conv_ln, b_conv_ln, g_out_att, g_out_conv, w_out, g_ffn,
              w_ff_gate, w_ff_up, w_ff_down, w_router, w_ex_gate, w_ex_up, w_ex_down,
              g_ple, w_ple_proj, w_ple_gate):

    def trunk(x, p, mix_fn):
        h = x
        states = []
        for i in range(DEPTH):
            q, k, v, logf, u = mixer_inputs(h, g_mix[i], w_in[i], b_forget[i], g_q[i], g_k[i])
            att, u_pad = mix_fn(i, q, k, v, logf, u)
            cnv = conv_module(u_pad, w_dw[i], b_dw[i], g_conv_ln[i], b_conv_ln[i])
            h = h + mixer_output(att, cnv, g_out_att[i], g_out_conv[i], w_out[i])
            n = rms_norm(h, g_ffn[i]).reshape(-1, D_MODEL)
            j = i // 2
            if i % 2 == 0:
                f = swiglu(n, w_ff_gate[j], w_ff_up[j], w_ff_down[j])
            else:
                f = moe_swiglu(n, w_router[j], w_ex_gate[j], w_ex_up[j], w_ex_down[j])
            h = h + f.reshape(h.shape)
            gate = jax.nn.sigmoid(rms_norm(h, g_ple[i]) @ w_ple_gate[i])
            h = h + (p[i].astype(h.dtype) @ w_ple_proj[i]) * gate
            states.append((k, v, logf, u_pad[:, -(CONV_WIDTH - 1):]))
        stacked = [jnp.stack(s) for s in zip(*states)]
        return h, stacked[0], stacked[1], stacked[2], stacked[3]

    def prompt_mix(i, q, k, v, logf, u):
        att = fox_prompt(q, k, v, logf)
        pad = jnp.zeros((u.shape[0], CONV_WIDTH - 1, CONV_CH), u.dtype)
        return att, jnp.concatenate([pad, u], axis=1)

    def sample_mix(i, q, k, v, logf, u):
        bd = q.shape[0]
        k_past = cache_k[i, page_table].reshape(bd, -1, N_HEADS, HEAD_DIM)
        v_past = cache_v[i, page_table].reshape(bd, -1, N_HEADS, HEAD_DIM)
        f_past = cache_logf[i, page_table].reshape(bd, -1, N_HEADS)
        att = fox_sample(q, k, v, logf, k_past, v_past, f_past)
        return att, jnp.concatenate([state_conv[i].astype(u.dtype), u], axis=1)

    y_prompt, k_prompt, v_prompt, logf_prompt, conv_prompt = trunk(x_prompt, p_prompt, prompt_mix)
    y_sample, k_sample, v_sample, logf_sample, conv_sample = trunk(x_sample, p_sample, sample_mix)
    return (y_prompt, y_sample, k_prompt, v_prompt, logf_prompt, conv_prompt,
            k_sample, v_sample, logf_sample, conv_sample)
```

```python
import functools

import jax
import jax.numpy as jnp
from jax import lax
from jax.experimental import pallas as pl
from jax.experimental.pallas import tpu as pltpu

F32 = jnp.float32
BF16 = jnp.bfloat16

D_MODEL = 1024
BATCH = 2
SEQ = 8192
DEPTH = 4
DEC_BATCH = 32
DEC_SEQ = 8
PAST_LEN = 8192
PAGE_SIZE = 128
N_PAGES = PAST_LEN // PAGE_SIZE
N_HEADS = 8
HEAD_DIM = 64
ATT_WIDTH = N_HEADS * HEAD_DIM
CONV_CH = D_MODEL // 2
CONV_WIDTH = 31
IN_WIDTH = 3 * ATT_WIDTH + 2 * CONV_CH + N_HEADS
D_FF = 2816
N_EXPERTS = 8
D_FF_EXPERT = D_MODEL * 7 // 2
PLE_DIM = 256
EPS = 1e-6
ATT_SCALE = HEAD_DIM ** -0.5

LANES = 128
N_PROMPT = BATCH * SEQ
N_SAMPLE = DEC_BATCH * DEC_SEQ
N_TOK = N_PROMPT + N_SAMPLE
TM = 256
NT = N_TOK // TM
NT_PROMPT = N_PROMPT // TM
TILES_PER_SEQ = SEQ // TM
IN_PAD = 3 * ATT_WIDTH + 2 * CONV_CH + LANES
HALO = 32
PAGES_PER_STEP = 8
KEYS_PER_STEP = PAGES_PER_STEP * PAGE_SIZE
STEPS_PER_SEQ = N_PAGES // PAGES_PER_STEP
MOE_TM = 1280
MOE_TF = 512
NEG = -1e30
VMEM_LIMIT = 56 * 1024 * 1024


def _params(sem, vmem=VMEM_LIMIT):
    return pltpu.CompilerParams(dimension_semantics=sem, vmem_limit_bytes=vmem)


def _const_spec(shape):
    nd = len(shape)
    return pl.BlockSpec(shape, lambda *_: (0,) * nd, pipeline_mode=pl.Buffered(1))


def _split3(x):
    hi = x.astype(BF16)
    r = x - hi.astype(F32)
    mid = r.astype(BF16)
    lo = (r - mid.astype(F32)).astype(BF16)
    return hi, mid, lo


def _rms(x, g):
    return x * lax.rsqrt(jnp.mean(x * x, axis=-1, keepdims=True) + EPS) * g


def _sigmoid(x):
    return 1.0 / (1.0 + jnp.exp(-x))


def _silu(x):
    return x * _sigmoid(x)


def _log_sigmoid(x):
    return jnp.minimum(x, 0.0) - jnp.log1p(jnp.exp(-jnp.abs(x)))


def _inproj_kernel(h_ref, gmix_ref, w_ref, bf_ref, gq_ref, gk_ref, bd_ref, tri_ref,
                   q4_ref, augq_ref, k4_ref, augk_ref, vt_ref, k_ref, v_ref, logf_ref, u_ref,
                   carry_ref):
    i = pl.program_id(0)
    xn = _rms(h_ref[...], gmix_ref[...]).astype(BF16)

    def proj(c0, c1):
        return jnp.dot(xn, w_ref[:, c0:c1], preferred_element_type=F32)

    def head_norm(z, g):
        ms = jnp.dot((z * z).astype(BF16), bd_ref[...], preferred_element_type=F32)
        return z * lax.rsqrt(ms + EPS) * g

    q = head_norm(proj(0, 512), gq_ref[...])
    k = head_norm(proj(512, 1024), gk_ref[...])
    v = proj(1024, 1536)
    ua = proj(1536, 2048)
    ug = proj(2048, 2560)
    zf = proj(2560, IN_PAD)

    lane = lax.broadcasted_iota(jnp.int32, (TM, LANES), 1)
    logf = jnp.where(lane < N_HEADS, _log_sigmoid(zf + bf_ref[...]), 0.0)

    k_ref[...] = k
    v_ref[...] = v
    u_ref[...] = ua * _sigmoid(ug)
    logf_ref[...] = logf[:, :N_HEADS]

    qb = q.astype(BF16)
    kb = k.astype(BF16)
    for j in range(4):
        q4_ref[0, j] = qb[:, j * LANES:(j + 1) * LANES]
        k4_ref[0, j] = kb[:, j * LANES:(j + 1) * LANES]
    vt_ref[0] = v.T.astype(BF16).reshape(N_HEADS, HEAD_DIM, TM)

    hi, mid, lo = _split3(logf)
    tri = tri_ref[...]
    c = (jnp.dot(tri, hi, preferred_element_type=F32)
         + jnp.dot(tri, mid, preferred_element_type=F32)
         + jnp.dot(tri, lo, preferred_element_type=F32))
    carry = jnp.where(i % TILES_PER_SEQ == 0, 0.0, carry_ref[0:1, :])
    c = c + carry
    carry_ref[...] = jnp.broadcast_to(c[TM - 1:TM, :], (8, LANES))

    chi, cmid, clo = (p.astype(F32) for p in _split3(c))
    ones_q = jnp.where((lane >= 24) & (lane < 48), 1.0, 0.0)
    ones_k = jnp.where(lane < 24, 1.0, 0.0)
    augq = chi + pltpu.roll(cmid, 8, 1) + pltpu.roll(clo, 16, 1) + ones_q
    augk = ones_k - pltpu.roll(chi, 24, 1) - pltpu.roll(cmid, 32, 1) - pltpu.roll(clo, 40, 1)
    augq_ref[...] = augq.astype(BF16)
    augk_ref[...] = augk.astype(BF16)


def _inproj(h, gmix, w, bfp, gq, gk, bd, tri):
    row = lambda i: (i, 0)
    out_shape = (
        jax.ShapeDtypeStruct((NT, 4, TM, LANES), BF16),
        jax.ShapeDtypeStruct((N_TOK, LANES), BF16),
        jax.ShapeDtypeStruct((NT, 4, TM, LANES), BF16),
        jax.ShapeDtypeStruct((N_TOK, LANES), BF16),
        jax.ShapeDtypeStruct((NT, N_HEADS, HEAD_DIM, TM), BF16),
        jax.ShapeDtypeStruct((N_TOK, ATT_WIDTH), F32),
        jax.ShapeDtypeStruct((N_TOK, ATT_WIDTH), F32),
        jax.ShapeDtypeStruct((N_TOK, N_HEADS), F32),
        jax.ShapeDtypeStruct((N_TOK, CONV_CH), F32),
    )
    out_specs = (
        pl.BlockSpec((1, 4, TM, LANES), lambda i: (i, 0, 0, 0)),
        pl.BlockSpec((TM, LANES), row),
        pl.BlockSpec((1, 4, TM, LANES), lambda i: (i, 0, 0, 0)),
        pl.BlockSpec((TM, LANES), row),
        pl.BlockSpec((1, N_HEADS, HEAD_DIM, TM), lambda i: (i, 0, 0, 0)),
        pl.BlockSpec((TM, ATT_WIDTH), row),
        pl.BlockSpec((TM, ATT_WIDTH), row),
        pl.BlockSpec((TM, N_HEADS), row),
        pl.BlockSpec((TM, CONV_CH), row),
    )
    return pl.pallas_call(
        _inproj_kernel,
        grid=(NT,),
        in_specs=[pl.BlockSpec((TM, D_MODEL), row),
                  _const_spec((1, D_MODEL)), _const_spec((D_MODEL, IN_PAD)), _const_spec((1, LANES)),
                  _const_spec((1, ATT_WIDTH)), _const_spec((1, ATT_WIDTH)),
                  _const_spec((ATT_WIDTH, ATT_WIDTH)), _const_spec((TM, TM))],
        out_specs=out_specs,
        out_shape=out_shape,
        scratch_shapes=[pltpu.VMEM((8, LANES), F32)],
        compiler_params=_params(("arbitrary",)),
        name="inproj",
    )(h, gmix, w, bfp, gq, gk, bd, tri)


def _attn_kernel(q4_ref, augq_ref, k4_ref, augk_ref, vt_ref, gatt_ref, o_ref, ot_scr):
    qi = pl.program_id(1)
    lane = lax.broadcasted_iota(jnp.int32, (TM, LANES), 1)
    augq = augq_ref[...]
    key_pos = lax.broadcasted_iota(jnp.int32, (TM, TM), 0)
    qry_pos = lax.broadcasted_iota(jnp.int32, (TM, TM), 1)
    causal = key_pos <= qry_pos

    def head_body(h, _):
        j = h // 2
        sub = h % 2
        zero = jnp.zeros((), BF16)
        qm = jnp.where((lane // HEAD_DIM) == sub, q4_ref[0, j], zero)
        am = jnp.where(((lane % 8) == h) & (lane < 48), augq, zero)
        qa = jnp.concatenate([qm, am], axis=1)

        def kv_step(kb, carry, masked):
            m, l, acc = carry
            ka = jnp.concatenate([k4_ref[kb, j], augk_ref[kb]], axis=1)
            st = lax.dot_general(ka, qa, (((1,), (1,)), ((), ())),
                                 preferred_element_type=F32)
            if masked:
                st = jnp.where(causal, st, NEG)
            m_new = jnp.maximum(m, jnp.max(st, axis=0, keepdims=True))
            alpha = jnp.exp(m - m_new)
            p = jnp.exp(st - m_new)
            l = alpha * l + jnp.sum(p, axis=0, keepdims=True)
            pv = jnp.dot(vt_ref[kb, h], p.astype(BF16), preferred_element_type=F32)
            return m_new, l, alpha * acc + pv

        init = (jnp.full((1, TM), NEG, F32), jnp.zeros((1, TM), F32), jnp.zeros((HEAD_DIM, TM), F32))
        carry = lax.fori_loop(0, qi, lambda kb, c: kv_step(kb, c, False), init)
        _, l, acc = kv_step(qi, carry, True)
        ot_scr[h] = acc / l
        return 0

    lax.fori_loop(0, N_HEADS, head_body, 0)
    o = ot_scr[...].reshape(ATT_WIDTH, TM).T
    o_ref[...] = _rms(o, gatt_ref[...]).astype(BF16)


def _attn_prompt(q4, augq, k4, augk3, vt, gatt):
    seq_blk = lambda b, qi: (b, 0, 0, 0)
    return pl.pallas_call(
        _attn_kernel,
        grid=(BATCH, TILES_PER_SEQ),
        in_specs=[pl.BlockSpec((1, 4, TM, LANES), lambda b, qi: (b * TILES_PER_SEQ + qi, 0, 0, 0)),
                  pl.BlockSpec((TM, LANES), lambda b, qi: (b * TILES_PER_SEQ + qi, 0)),
                  pl.BlockSpec((TILES_PER_SEQ, 4, TM, LANES), seq_blk),
                  pl.BlockSpec((TILES_PER_SEQ, TM, LANES), lambda b, qi: (b, 0, 0)),
                  pl.BlockSpec((TILES_PER_SEQ, N_HEADS, HEAD_DIM, TM), seq_blk),
                  _const_spec((1, ATT_WIDTH))],
        out_specs=pl.BlockSpec((TM, ATT_WIDTH), lambda b, qi: (b * TILES_PER_SEQ + qi, 0)),
        out_shape=jax.ShapeDtypeStruct((N_PROMPT, ATT_WIDTH), BF16),
        scratch_shapes=[pltpu.VMEM((N_HEADS, HEAD_DIM, TM), F32)],
        compiler_params=_params(("arbitrary", "arbitrary")),
        name="attn_prompt",
    )(q4, augq, k4, augk3, vt, gatt)


def _suffix_kernel(x_ref, upper_ref, pages_ref, hi_ref, mid_ref, lo_ref):
    x = x_ref[0]
    pieces = _split3(x)
    upper = upper_ref[...]
    within = sum(jnp.dot(upper, p, preferred_element_type=F32) for p in pieces)
    page_tot = within[0:1, :] + x[0:1, :]
    tot8 = jnp.broadcast_to(page_tot, (8, N_PAGES * N_HEADS))
    later = sum(jnp.dot(p, pages_ref[...], preferred_element_type=F32) for p in _split3(tot8))
    d = within + later[0:1, :]
    hi, mid, lo = _split3(d)
    hi_ref[0] = hi
    mid_ref[0] = mid
    lo_ref[0] = lo


def _suffix(x, upper, pages):
    width = N_PAGES * N_HEADS
    blk = pl.BlockSpec((1, PAGE_SIZE, width), lambda b: (b, 0, 0))
    shp = jax.ShapeDtypeStruct((DEC_BATCH, PAGE_SIZE, width), BF16)
    return pl.pallas_call(
        _suffix_kernel,
        grid=(DEC_BATCH,),
        in_specs=[blk, _const_spec((PAGE_SIZE, PAGE_SIZE)), _const_spec((width, width))],
        out_specs=(blk, blk, blk),
        out_shape=(shp, shp, shp),
        compiler_params=_params(("arbitrary",)),
        name="suffix_logf",
    )(x, upper, pages)


def _attn_sample_kernel(pt_ref, *refs):
    del pt_ref
    kp = refs[:PAGES_PER_STEP]
    vp = refs[PAGES_PER_STEP:2 * PAGES_PER_STEP]
    (dp_ref, qe_ref, cb_ref, knew_ref, vnew_ref, ynew_ref, gatt_ref,
     o_ref, m_scr, l_scr, acc_scr) = refs[2 * PAGES_PER_STEP:]
    s = pl.program_id(1)
    qe = qe_ref[0]
    cb = cb_ref[0]

    def update(st, vmat, m, l, acc):
        m_new = jnp.maximum(m, jnp.max(st, axis=0, keepdims=True))
        alpha = jnp.exp(m - m_new)
        p = jnp.exp(st - m_new)
        l = alpha * l + jnp.sum(p, axis=0, keepdims=True)
        pv = jnp.dot(p.T.astype(BF16), vmat, preferred_element_type=F32)
        alpha_col = jnp.broadcast_to(alpha, (LANES, LANES)).T
        acc = acc * jnp.concatenate([alpha_col] * 4, axis=1) + pv
        return m_new, l, acc

    @pl.when(s == 0)
    def _():
        kn = jnp.concatenate([knew_ref[0].astype(BF16), jnp.zeros((LANES - DEC_SEQ, ATT_WIDTH), BF16)], axis=0)
        vn = jnp.concatenate([vnew_ref[0].astype(BF16), jnp.zeros((LANES - DEC_SEQ, ATT_WIDTH), BF16)], axis=0)
        st = jnp.dot(kn, qe[:ATT_WIDTH], preferred_element_type=F32)
        yn = jnp.concatenate([ynew_ref[0], jnp.zeros((LANES - DEC_SEQ, LANES), F32)], axis=0)
        st = st + cb - yn
        key = lax.broadcasted_iota(jnp.int32, (LANES, LANES), 0)
        col = lax.broadcasted_iota(jnp.int32, (LANES, LANES), 1)
        st = jnp.where((key < DEC_SEQ) & (key <= col % DEC_SEQ), st, NEG)
        m, l, acc = update(st, vn, jnp.full((1, LANES), NEG, F32), jnp.zeros((1, LANES), F32),
                           jnp.zeros((LANES, ATT_WIDTH), F32))
        m_scr[...] = m
        l_scr[...] = l
        acc_scr[...] = acc

    kmat = jnp.concatenate([r[0, 0].astype(BF16) for r in kp], axis=0)
    vmat = jnp.concatenate([r[0, 0].astype(BF16) for r in vp], axis=0)
    lhs = jnp.concatenate([kmat, dp_ref[0]], axis=1)
    st = jnp.dot(lhs, qe, preferred_element_type=F32) + cb
    m, l, acc = update(st, vmat, m_scr[...], l_scr[...], acc_scr[...])
    m_scr[...] = m
    l_scr[...] = l
    acc_scr[...] = acc

    @pl.when(s == STEPS_PER_SEQ - 1)
    def _():
        linv_col = jnp.broadcast_to(1.0 / l, (LANES, LANES)).T
        o = acc * jnp.concatenate([linv_col] * 4, axis=1)
        lane = lax.broadcasted_iota(jnp.int32, (DEC_SEQ, ATT_WIDTH), 1)
        att = jnp.zeros((DEC_SEQ, ATT_WIDTH), F32)
        for h in range(N_HEADS):
            att = att + jnp.where(lane // HEAD_DIM == h, o[h * DEC_SEQ:(h + 1) * DEC_SEQ], 0.0)
        o_ref[...] = _rms(att, gatt_ref[...])


def _attn_sample(layer, page_table, cache_k4, cache_v4, dp, qe, cb, knew, vnew, ynew, gatt):
    def page_spec(j):
        return pl.BlockSpec((1, 1, PAGE_SIZE, ATT_WIDTH),
                            lambda b, s, pt: (layer, pt[b, s * PAGES_PER_STEP + j], 0, 0))

    per_seq = lambda shape: pl.BlockSpec((1,) + shape, lambda b, s, pt: (b, 0, 0))
    grid_spec = pltpu.PrefetchScalarGridSpec(
        num_scalar_prefetch=1,
        grid=(DEC_BATCH, STEPS_PER_SEQ),
        in_specs=([page_spec(j) for j in range(PAGES_PER_STEP)]
                  + [page_spec(j) for j in range(PAGES_PER_STEP)]
                  + [pl.BlockSpec((1, KEYS_PER_STEP, LANES), lambda b, s, pt: (b, s, 0)),
                     per_seq((ATT_WIDTH + LANES, LANES)), per_seq((1, LANES)),
                     per_seq((DEC_SEQ, ATT_WIDTH)), per_seq((DEC_SEQ, ATT_WIDTH)),
                     per_seq((DEC_SEQ, LANES)),
                     pl.BlockSpec((1, ATT_WIDTH), lambda b, s, pt: (0, 0))]),
        out_specs=pl.BlockSpec((DEC_SEQ, ATT_WIDTH), lambda b, s, pt: (b, 0)),
        scratch_shapes=[pltpu.VMEM((1, LANES), F32), pltpu.VMEM((1, LANES), F32),
                        pltpu.VMEM((LANES, ATT_WIDTH), F32)],
    )
    return pl.pallas_call(
        _attn_sample_kernel,
        grid_spec=grid_spec,
        out_shape=jax.ShapeDtypeStruct((N_SAMPLE, ATT_WIDTH), F32),
        compiler_params=_params(("arbitrary", "arbitrary")),
        name="attn_sample",
    )(page_table, *([cache_k4] * PAGES_PER_STEP), *([cache_v4] * PAGES_PER_STEP),
      dp, qe, cb, knew, vnew, ynew, gatt)


def _conv_kernel(u_ref, prev_ref, state_ref, w_ref, bdw_ref, gln_ref, bln_ref, gcv_ref,
                 o_ref, xs_scr, xs3_scr, y_scr):
    t = pl.program_id(0)

    @pl.when(t < NT_PROMPT)
    def _():
        first = (t % TILES_PER_SEQ) == 0
        xs_scr[0:HALO, :] = jnp.where(first, 0.0, prev_ref[...])
        xs_scr[HALO:, :] = u_ref[...]
        acc = jnp.zeros((TM, CONV_CH), F32)
        for w in range(CONV_WIDTH):
            off = w + HALO - (CONV_WIDTH - 1)
            acc = acc + xs_scr[off:off + TM, :] * w_ref[w:w + 1, :]
        y_scr[...] = acc

    @pl.when(t == NT_PROMPT)
    def _():
        xs3_scr[:, 0:HALO, :] = state_ref[...]
        xs3_scr[:, HALO:, :] = u_ref[...].reshape(DEC_BATCH, DEC_SEQ, CONV_CH)
        acc = jnp.zeros((DEC_BATCH, DEC_SEQ, CONV_CH), F32)
        for w in range(CONV_WIDTH):
            off = w + HALO - (CONV_WIDTH - 1)
            acc = acc + xs3_scr[:, off:off + DEC_SEQ, :] * w_ref[w:w + 1, :]
        y_scr[...] = acc.reshape(TM, CONV_CH)

    y = y_scr[...] + bdw_ref[...]
    yc = y - jnp.mean(y, axis=-1, keepdims=True)
    yn = yc * lax.rsqrt(jnp.mean(yc * yc, axis=-1, keepdims=True) + EPS) * gln_ref[...] + bln_ref[...]
    o_ref[...] = _rms(_silu(yn), gcv_ref[...]).astype(BF16)


def _conv(u, state_pad, w, bdw, gln, bln, gcv):
    halo_blocks = TM // HALO
    return pl.pallas_call(
        _conv_kernel,
        grid=(NT,),
        in_specs=[pl.BlockSpec((TM, CONV_CH), lambda t: (t, 0)),
                  pl.BlockSpec((HALO, CONV_CH), lambda t: (jnp.maximum(t * halo_blocks - 1, 0), 0)),
                  _const_spec((DEC_BATCH, HALO, CONV_CH)),
                  _const_spec((HALO, CONV_CH)),
                  _const_spec((1, CONV_CH)), _const_spec((1, CONV_CH)),
                  _const_spec((1, CONV_CH)), _const_spec((1, CONV_CH))],
        out_specs=pl.BlockSpec((TM, CONV_CH), lambda t: (t, 0)),
        out_shape=jax.ShapeDtypeStruct((N_TOK, CONV_CH), BF16),
        scratch_shapes=[pltpu.VMEM((TM + HALO, CONV_CH), F32),
                        pltpu.VMEM((DEC_BATCH, HALO + DEC_SEQ, CONV_CH), F32),
                        pltpu.VMEM((TM, CONV_CH), F32)],
        compiler_params=_params(("arbitrary",)),
        name="conv",
    )(u, u, state_pad, w, bdw, gln, bln, gcv)


def _top2_gates(logits):
    lane = lax.broadcasted_iota(jnp.int32, logits.shape, 1)
    m1 = jnp.max(logits, axis=-1, keepdims=True)
    i1 = jnp.min(jnp.where(logits == m1, lane, LANES), axis=-1, keepdims=True)
    rest = jnp.where(lane == i1, NEG, logits)
    m2 = jnp.max(rest, axis=-1, keepdims=True)
    i2 = jnp.min(jnp.where(rest == m2, lane, LANES), axis=-1, keepdims=True)
    e = jnp.exp(m2 - m1)
    w1 = 1.0 / (1.0 + e)
    w2 = e / (1.0 + e)
    return jnp.where(lane == i1, w1, 0.0) + jnp.where(lane == i2, w2, 0.0)


def _outproj_kernel(*refs, with_router):
    if with_router:
        (h_ref, attp_ref, atts_ref, cn_ref, wa_ref, wc_ref, gffn_ref, wr_hi_ref, wr_lo_ref,
         h1_ref, n_ref, gates_ref) = refs
    else:
        h_ref, attp_ref, atts_ref, cn_ref, wa_ref, wc_ref, gffn_ref, h1_ref, n_ref = refs
    t = pl.program_id(0)
    att = jnp.where(t == NT_PROMPT, atts_ref[...].astype(BF16), attp_ref[...])
    h1 = (h_ref[...]
          + jnp.dot(att, wa_ref[...], preferred_element_type=F32)
          + jnp.dot(cn_ref[...], wc_ref[...], preferred_element_type=F32))
    h1_ref[...] = h1
    n = _rms(h1, gffn_ref[...])
    n_ref[...] = n.astype(BF16)
    if with_router:
        n_hi = n.astype(BF16)
        n_lo = (n - n_hi.astype(F32)).astype(BF16)
        logits = (jnp.dot(n_hi, wr_hi_ref[...], preferred_element_type=F32)
                  + jnp.dot(n_hi, wr_lo_ref[...], preferred_element_type=F32)
                  + jnp.dot(n_lo, wr_hi_ref[...], preferred_element_type=F32))
        lane = lax.broadcasted_iota(jnp.int32, logits.shape, 1)
        gates_ref[...] = _top2_gates(jnp.where(lane < N_EXPERTS, logits, NEG))


def _outproj(h, attp, atts, cn, wa, wc, gffn, router=None):
    row = lambda t: (t, 0)
    in_specs = [pl.BlockSpec((TM, D_MODEL), row),
                pl.BlockSpec((TM, ATT_WIDTH), lambda t: (jnp.minimum(t, NT_PROMPT - 1), 0)),
                _const_spec((N_SAMPLE, ATT_WIDTH)),
                pl.BlockSpec((TM, CONV_CH), row),
                _const_spec((ATT_WIDTH, D_MODEL)), _const_spec((CONV_CH, D_MODEL)),
                _const_spec((1, D_MODEL))]
    out_specs = [pl.BlockSpec((TM, D_MODEL), row), pl.BlockSpec((TM, D_MODEL), row)]
    out_shape = [jax.ShapeDtypeStruct((N_TOK, D_MODEL), F32), jax.ShapeDtypeStruct((N_TOK, D_MODEL), BF16)]
    args = [h, attp, atts, cn, wa, wc, gffn]
    if router is not None:
        in_specs += [_const_spec((D_MODEL, LANES)), _const_spec((D_MODEL, LANES))]
        out_specs.append(pl.BlockSpec((TM, LANES), row))
        out_shape.append(jax.ShapeDtypeStruct((N_TOK, LANES), F32))
        args += list(router)
    return pl.pallas_call(
        functools.partial(_outproj_kernel, with_router=router is not None),
        grid=(NT,),
        in_specs=in_specs,
        out_specs=tuple(out_specs),
        out_shape=tuple(out_shape),
        compiler_params=_params(("arbitrary",)),
        name="outproj",
    )(*args)


def _ple(h2, p_ref, gple_ref, wproj_ref, wgate_ref):
    gate = _sigmoid(jnp.dot(_rms(h2, gple_ref[...]).astype(BF16), wgate_ref[...],
                            preferred_element_type=F32))
    proj = jnp.dot(p_ref[...].astype(BF16), wproj_ref[...], preferred_element_type=F32)
    return h2 + proj * gate


def _ffn_ple_kernel(h1_ref, n_ref, wg_ref, wu_ref, wd_ref, p_ref, gple_ref, wproj_ref, wgate_ref, o_ref):
    n = n_ref[...]
    g = jnp.dot(n, wg_ref[...], preferred_element_type=F32)
    u = jnp.dot(n, wu_ref[...], preferred_element_type=F32)
    a = (_silu(g) * u).astype(BF16)
    h2 = h1_ref[...] + jnp.dot(a, wd_ref[...], preferred_element_type=F32)
    o_ref[...] = _ple(h2, p_ref, gple_ref, wproj_ref, wgate_ref)


def _ffn_ple(h1, n, wg, wu, wd, p, gple, wproj, wgate):
    row = lambda t: (t, 0)
    return pl.pallas_call(
        _ffn_ple_kernel,
        grid=(NT,),
        in_specs=[pl.BlockSpec((TM, D_MODEL), row), pl.BlockSpec((TM, D_MODEL), row),
                  _const_spec((D_MODEL, D_FF)), _const_spec((D_MODEL, D_FF)), _const_spec((D_FF, D_MODEL)),
                  pl.BlockSpec((TM, PLE_DIM), row), _const_spec((1, D_MODEL)),
                  _const_spec((PLE_DIM, D_MODEL)), _const_spec((D_MODEL, D_MODEL))],
        out_specs=pl.BlockSpec((TM, D_MODEL), row),
        out_shape=jax.ShapeDtypeStruct((N_TOK, D_MODEL), F32),
        compiler_params=_params(("arbitrary",)),
        name="ffn_ple",
    )(h1, n, wg, wu, wd, p, gple, wproj, wgate)


def _moe_kernel(h1_ref, n_ref, gates_ref, wg_ref, wu_ref, wd_ref, o_ref, acc_scr):
    e = pl.program_id(1)
    c = pl.program_id(2)

    @pl.when((e == 0) & (c == 0))
    def _():
        acc_scr[...] = h1_ref[...]

    lane = lax.broadcasted_iota(jnp.int32, (MOE_TM, LANES), 1)
    gate = jnp.sum(jnp.where(lane == e, gates_ref[...], 0.0), axis=-1, keepdims=True)
    n = n_ref[...]
    g = jnp.dot(n, wg_ref[0], preferred_element_type=F32)
    u = jnp.dot(n, wu_ref[0], preferred_element_type=F32)
    a = (_silu(g) * u * gate).astype(BF16)
    acc_scr[...] += jnp.dot(a, wd_ref[0], preferred_element_type=F32)

    @pl.when((e == N_EXPERTS - 1) & (c == pl.num_programs(2) - 1))
    def _():
        o_ref[...] = acc_scr[...]


def _moe(h1, n, gates, wg, wu, wd):
    row = lambda t, e, c: (t, 0)
    return pl.pallas_call(
        _moe_kernel,
        grid=(N_TOK // MOE_TM, N_EXPERTS, D_FF_EXPERT // MOE_TF),
        in_specs=[pl.BlockSpec((MOE_TM, D_MODEL), row), pl.BlockSpec((MOE_TM, D_MODEL), row),
                  pl.BlockSpec((MOE_TM, LANES), row),
                  pl.BlockSpec((1, D_MODEL, MOE_TF), lambda t, e, c: (e, 0, c)),
                  pl.BlockSpec((1, D_MODEL, MOE_TF), lambda t, e, c: (e, 0, c)),
                  pl.BlockSpec((1, MOE_TF, D_MODEL), lambda t, e, c: (e, c, 0))],
        out_specs=pl.BlockSpec((MOE_TM, D_MODEL), row),
        out_shape=jax.ShapeDtypeStruct((N_TOK, D_MODEL), F32),
        scratch_shapes=[pltpu.VMEM((MOE_TM, D_MODEL), F32)],
        compiler_params=_params(("arbitrary", "arbitrary", "arbitrary")),
        name="moe",
    )(h1, n, gates, wg, wu, wd)


def _ple_kernel(h2_ref, p_ref, gple_ref, wproj_ref, wgate_ref, o_ref):
    o_ref[...] = _ple(h2_ref[...], p_ref, gple_ref, wproj_ref, wgate_ref)


def _ple_call(h2, p, gple, wproj, wgate):
    row = lambda t: (t, 0)
    return pl.pallas_call(
        _ple_kernel,
        grid=(NT,),
        in_specs=[pl.BlockSpec((TM, D_MODEL), row), pl.BlockSpec((TM, PLE_DIM), row),
                  _const_spec((1, D_MODEL)), _const_spec((PLE_DIM, D_MODEL)),
                  _const_spec((D_MODEL, D_MODEL))],
        out_specs=pl.BlockSpec((TM, D_MODEL), row),
        out_shape=jax.ShapeDtypeStruct((N_TOK, D_MODEL), F32),
        compiler_params=_params(("arbitrary",)),
        name="ple",
    )(h2, p, gple, wproj, wgate)


def _newkey_kernel(logf_ref, expand_ref, y_ref, cb_ref):
    lf = logf_ref[0]
    row = lax.broadcasted_iota(jnp.int32, (DEC_SEQ, LANES), 0)
    cn = jnp.zeros((DEC_SEQ, LANES), F32)
    for r in range(DEC_SEQ):
        cn = cn + jnp.where(row >= r, lf[r:r + 1, :], 0.0)
    y = sum(jnp.dot(p, expand_ref[...], preferred_element_type=F32) for p in _split3(cn))
    y_ref[0] = y
    lane = lax.broadcasted_iota(jnp.int32, (DEC_SEQ, LANES), 1)
    cb_ref[0] = jnp.sum(jnp.where(lane % DEC_SEQ == row, y, 0.0), axis=0, keepdims=True)


def _newkey(logf_pad, expand):
    blk = lambda shape: pl.BlockSpec((1,) + shape, lambda b: (b, 0, 0))
    return pl.pallas_call(
        _newkey_kernel,
        grid=(DEC_BATCH,),
        in_specs=[blk((DEC_SEQ, LANES)), _const_spec((LANES, LANES))],
        out_specs=(blk((DEC_SEQ, LANES)), blk((1, LANES))),
        out_shape=(jax.ShapeDtypeStruct((DEC_BATCH, DEC_SEQ, LANES), F32),
                   jax.ShapeDtypeStruct((DEC_BATCH, 1, LANES), F32)),
        compiler_params=_params(("arbitrary",)),
        name="newkey_bias",
    )(logf_pad, expand)


def kernel(x_prompt, x_sample, cache_k, cache_v, cache_logf, state_conv, page_table, p_prompt, p_sample, g_mix, w_in, b_forget, g_q, g_k, w_dw, b_dw, g_conv_ln, b_conv_ln, g_out_att, g_out_conv, w_out, g_ffn, w_ff_gate, w_ff_up, w_ff_down, w_router, w_ex_gate, w_ex_up, w_ex_down, g_ple, w_ple_proj, w_ple_gate):
    n_pool = cache_k.shape[1]
    h = jnp.concatenate([x_prompt.reshape(N_PROMPT, D_MODEL), x_sample.reshape(N_SAMPLE, D_MODEL)], axis=0)
    cache_k4 = cache_k.reshape(DEPTH, n_pool, PAGE_SIZE, ATT_WIDTH)
    cache_v4 = cache_v.reshape(DEPTH, n_pool, PAGE_SIZE, ATT_WIDTH)

    idx = jnp.arange(ATT_WIDTH)
    bd = jnp.where(idx[:, None] // HEAD_DIM == idx[None, :] // HEAD_DIM, 1.0 / HEAD_DIM, 0.0).astype(BF16)
    r = jnp.arange(TM)
    tri = (r[None, :] <= r[:, None]).astype(BF16)
    rp = jnp.arange(PAGE_SIZE)
    upper = (rp[None, :] > rp[:, None]).astype(BF16)
    cw = jnp.arange(N_PAGES * N_HEADS)
    pages = ((cw[:, None] % N_HEADS == cw[None, :] % N_HEADS)
             & (cw[:, None] // N_HEADS > cw[None, :] // N_HEADS)).astype(BF16)
    ln = jnp.arange(LANES)
    col_head = jnp.where(ln < N_HEADS * DEC_SEQ, ln // DEC_SEQ, -1)
    expand = ((ln[:, None] == col_head[None, :]) & (ln[:, None] < N_HEADS)).astype(BF16)
    e_rows = jnp.where(ln < 3 * N_HEADS, ln % N_HEADS, -2)
    e_part = (e_rows[:, None] == col_head[None, :]).astype(BF16)
    head_of_row = jnp.arange(ATT_WIDTH) // HEAD_DIM
    q_mask = (head_of_row[:, None] == col_head[None, :])

    states = []
    for i in range(DEPTH):
        w_pad = jnp.pad(w_in[i], ((0, 0), (0, IN_PAD - IN_WIDTH))).astype(BF16)
        bfp = jnp.pad(b_forget[i], (0, LANES - N_HEADS)).reshape(1, LANES)
        gq = (jnp.tile(g_q[i], N_HEADS) * ATT_SCALE).reshape(1, ATT_WIDTH)
        gk = jnp.tile(g_k[i], N_HEADS).reshape(1, ATT_WIDTH)
        q4, augq, k4, augk, vt, k, v, logf, u = _inproj(
            h, g_mix[i].reshape(1, D_MODEL), w_pad, bfp, gq, gk, bd, tri)
        gatt = g_out_att[i].reshape(1, ATT_WIDTH)

        attp = _attn_prompt(q4, augq, k4, augk.reshape(NT, TM, LANES), vt, gatt)

        f_past = cache_logf[i][page_table]
        x_suf = f_past.transpose(0, 2, 1, 3).reshape(DEC_BATCH, PAGE_SIZE, N_PAGES * N_HEADS)
        d_pieces = _suffix(x_suf, upper, pages)
        dp = jnp.concatenate(
            [d.reshape(DEC_BATCH, PAGE_SIZE, N_PAGES, N_HEADS).transpose(0, 2, 1, 3)
              .reshape(DEC_BATCH, PAST_LEN, N_HEADS) for d in d_pieces], axis=-1)
        dp = jnp.pad(dp, ((0, 0), (0, 0), (0, LANES - 3 * N_HEADS)))
        q_s = q4[NT_PROMPT].transpose(1, 0, 2).reshape(DEC_BATCH, DEC_SEQ, ATT_WIDTH)
        q_cols = jnp.tile(q_s.transpose(0, 2, 1), (1, 1, N_HEADS))
        q_cols = jnp.pad(q_cols, ((0, 0), (0, 0), (0, LANES - N_HEADS * DEC_SEQ)))
        qblk = jnp.where(q_mask[None], q_cols, jnp.zeros((), BF16))
        qe = jnp.concatenate([qblk, jnp.broadcast_to(e_part, (DEC_BATCH, LANES, LANES))], axis=1)
        logf_s = jnp.pad(logf[N_PROMPT:], ((0, 0), (0, LANES - N_HEADS))).reshape(DEC_BATCH, DEC_SEQ, LANES)
        ynew, cb = _newkey(logf_s, expand)
        k_s = k[N_PROMPT:].reshape(DEC_BATCH, DEC_SEQ, ATT_WIDTH)
        v_s = v[N_PROMPT:].reshape(DEC_BATCH, DEC_SEQ, ATT_WIDTH)
        atts = _attn_sample(i, page_table, cache_k4, cache_v4, dp, qe, cb, k_s, v_s, ynew, gatt)

        state_pad = jnp.pad(state_conv[i], ((0, 0), (HALO - (CONV_WIDTH - 1), 0), (0, 0)))
        w_dw_pad = jnp.pad(w_dw[i], ((0, HALO - CONV_WIDTH), (0, 0)))
        cn = _conv(u, state_pad, w_dw_pad, b_dw[i].reshape(1, CONV_CH), g_conv_ln[i].reshape(1, CONV_CH),
                   b_conv_ln[i].reshape(1, CONV_CH), g_out_conv[i].reshape(1, CONV_CH))

        wo = w_out[i].astype(BF16)
        p_all = jnp.concatenate([p_prompt[i].reshape(N_PROMPT, PLE_DIM), p_sample[i].reshape(N_SAMPLE, PLE_DIM)], axis=0)
        gple = g_ple[i].reshape(1, D_MODEL)
        wproj = w_ple_proj[i].astype(BF16)
        wgate = w_ple_gate[i].astype(BF16)
        gffn = g_ffn[i].reshape(1, D_MODEL)
        j = i // 2
        if i % 2 == 0:
            h1, n = _outproj(h, attp, atts, cn, wo[:ATT_WIDTH], wo[ATT_WIDTH:], gffn)
            h = _ffn_ple(h1, n, w_ff_gate[j].astype(BF16), w_ff_up[j].astype(BF16),
                         w_ff_down[j].astype(BF16), p_all, gple, wproj, wgate)
        else:
            wr = jnp.pad(w_router[j], ((0, 0), (0, LANES - N_EXPERTS)))
            wr_hi = wr.astype(BF16)
            wr_lo = (wr - wr_hi.astype(F32)).astype(BF16)
            h1, n, gates = _outproj(h, attp, atts, cn, wo[:ATT_WIDTH], wo[ATT_WIDTH:], gffn,
                                    router=(wr_hi, wr_lo))
            h2 = _moe(h1, n, gates, w_ex_gate[j].astype(BF16), w_ex_up[j].astype(BF16),
                      w_ex_down[j].astype(BF16))
            h = _ple_call(h2, p_all, gple, wproj, wgate)

        conv_p = u[:N_PROMPT].reshape(BATCH, SEQ, CONV_CH)[:, SEQ - (CONV_WIDTH - 1):]
        conv_s = jnp.concatenate([state_conv[i], u[N_PROMPT:].reshape(DEC_BATCH, DEC_SEQ, CONV_CH)],
                                 axis=1)[:, DEC_SEQ:]
        states.append((k, v, logf, conv_p, conv_s))

    def stack(idx, lo, hi, shape):
        return jnp.stack([s[idx][lo:hi].reshape(shape) for s in states])

    kv_p = (BATCH, SEQ, N_HEADS, HEAD_DIM)
    kv_s = (DEC_BATCH, DEC_SEQ, N_HEADS, HEAD_DIM)
    return (h[:N_PROMPT].reshape(BATCH, SEQ, D_MODEL),
            h[N_PROMPT:].reshape(DEC_BATCH, DEC_SEQ, D_MODEL),
            stack(0, 0, N_PROMPT, kv_p), stack(1, 0, N_PROMPT, kv_p),
            stack(2, 0, N_PROMPT, (BATCH, SEQ, N_HEADS)),
            jnp.stack([s[3] for s in states]),
            stack(0, N_PROMPT, N_TOK, kv_s), stack(1, N_PROMPT, N_TOK, kv_s),
            stack(2, N_PROMPT, N_TOK, (DEC_BATCH, DEC_SEQ, N_HEADS)),
            jnp.stack([s[4] for s in states]))
```

```python
import functools

import jax
import jax.numpy as jnp
from jax import lax
from jax.experimental import pallas as pl
from jax.experimental.pallas import tpu as pltpu

F32 = jnp.float32
BF16 = jnp.bfloat16

D_MODEL = 1024
BATCH = 2
SEQ = 8192
DEPTH = 4
DEC_BATCH = 32
DEC_SEQ = 8
PAST_LEN = 8192
PAGE_SIZE = 128
N_PAGES = PAST_LEN // PAGE_SIZE
N_HEADS = 8
HEAD_DIM = 64
ATT_WIDTH = N_HEADS * HEAD_DIM
CONV_CH = D_MODEL // 2
CONV_WIDTH = 31
IN_WIDTH = 3 * ATT_WIDTH + 2 * CONV_CH + N_HEADS
D_FF = 2816
N_EXPERTS = 8
D_FF_EXPERT = D_MODEL * 7 // 2
PLE_DIM = 256
EPS = 1e-6
ATT_SCALE = HEAD_DIM ** -0.5

LANES = 128
N_PROMPT = BATCH * SEQ
N_SAMPLE = DEC_BATCH * DEC_SEQ
N_TOK = N_PROMPT + N_SAMPLE
TM = 256
NT = N_TOK // TM
NT_PROMPT = N_PROMPT // TM
TILES_PER_SEQ = SEQ // TM
IN_PAD = 3 * ATT_WIDTH + 2 * CONV_CH + LANES
VT_ROWS = HEAD_DIM + 16
HALO = 32
PAGES_PER_STEP = 8
KEYS_PER_STEP = PAGES_PER_STEP * PAGE_SIZE
STEPS_PER_SEQ = N_PAGES // PAGES_PER_STEP
MOE_TM = 1280
MOE_TF = 896
MOE_CH = 128
NEG = -1e30
VMEM_LIMIT = 56 * 1024 * 1024


def _params(sem, vmem=VMEM_LIMIT):
    return pltpu.CompilerParams(dimension_semantics=sem, vmem_limit_bytes=vmem)


def _const_spec(shape):
    nd = len(shape)
    return pl.BlockSpec(shape, lambda *_: (0,) * nd, pipeline_mode=pl.Buffered(1))


def _split3(x):
    hi = x.astype(BF16)
    r = x - hi.astype(F32)
    mid = r.astype(BF16)
    lo = (r - mid.astype(F32)).astype(BF16)
    return hi, mid, lo


def _rms(x, g):
    return x * lax.rsqrt(jnp.mean(x * x, axis=-1, keepdims=True) + EPS) * g


def _sigmoid(x):
    return 1.0 / (1.0 + jnp.exp(-x))


def _silu(x):
    return x * _sigmoid(x)


def _log_sigmoid(x):
    return jnp.minimum(x, 0.0) - jnp.log1p(jnp.exp(-jnp.abs(x)))


def _inproj_kernel(h_ref, gmix_ref, w_ref, bf_ref, gq_ref, gk_ref, bd_ref, tri_ref,
                   q4_ref, augq_ref, k4_ref, augk_ref, vt_ref, k_ref, v_ref, logf_ref, u_ref,
                   carry_ref):
    i = pl.program_id(0)
    xn = _rms(h_ref[...], gmix_ref[...]).astype(BF16)

    def proj(c0, c1):
        return jnp.dot(xn, w_ref[:, c0:c1], preferred_element_type=F32)

    def head_norm(z, g):
        ms = jnp.dot((z * z).astype(BF16), bd_ref[...], preferred_element_type=F32)
        return z * lax.rsqrt(ms + EPS) * g

    q = head_norm(proj(0, 512), gq_ref[...])
    k = head_norm(proj(512, 1024), gk_ref[...])
    v = proj(1024, 1536)
    ua = proj(1536, 2048)
    ug = proj(2048, 2560)
    zf = proj(2560, IN_PAD)

    lane = lax.broadcasted_iota(jnp.int32, (TM, LANES), 1)
    logf = jnp.where(lane < N_HEADS, _log_sigmoid(zf + bf_ref[...]), 0.0)

    k_ref[...] = k
    v_ref[...] = v
    u_ref[...] = ua * _sigmoid(ug)
    logf_ref[...] = logf[:, :N_HEADS]

    qb = q.astype(BF16)
    kb = k.astype(BF16)
    for j in range(4):
        q4_ref[0, j] = qb[:, j * LANES:(j + 1) * LANES]
        k4_ref[0, j] = kb[:, j * LANES:(j + 1) * LANES]
    vt_ref[0, :, 0:HEAD_DIM, :] = v.T.astype(BF16).reshape(N_HEADS, HEAD_DIM, TM)
    vt_ref[0, :, HEAD_DIM:VT_ROWS, :] = jnp.ones((N_HEADS, VT_ROWS - HEAD_DIM, TM), BF16)

    hi, mid, lo = _split3(logf)
    tri = tri_ref[...]
    c = (jnp.dot(tri, hi, preferred_element_type=F32)
         + jnp.dot(tri, mid, preferred_element_type=F32)
         + jnp.dot(tri, lo, preferred_element_type=F32))
    carry = jnp.where(i % TILES_PER_SEQ == 0, 0.0, carry_ref[0:1, :])
    c = c + carry
    carry_ref[...] = jnp.broadcast_to(c[TM - 1:TM, :], (8, LANES))

    chi, cmid, clo = (p.astype(F32) for p in _split3(c))
    ones_q = jnp.where((lane >= 24) & (lane < 48), 1.0, 0.0)
    ones_k = jnp.where(lane < 24, 1.0, 0.0)
    augq = chi + pltpu.roll(cmid, 8, 1) + pltpu.roll(clo, 16, 1) + ones_q
    augk = ones_k - pltpu.roll(chi, 24, 1) - pltpu.roll(cmid, 32, 1) - pltpu.roll(clo, 40, 1)
    augq_ref[...] = augq.astype(BF16)
    augk_ref[...] = augk.astype(BF16)


def _inproj(h, gmix, w, bfp, gq, gk, bd, tri):
    row = lambda i: (i, 0)
    out_shape = (
        jax.ShapeDtypeStruct((NT, 4, TM, LANES), BF16),
        jax.ShapeDtypeStruct((N_TOK, LANES), BF16),
        jax.ShapeDtypeStruct((NT, 4, TM, LANES), BF16),
        jax.ShapeDtypeStruct((N_TOK, LANES), BF16),
        jax.ShapeDtypeStruct((NT, N_HEADS, VT_ROWS, TM), BF16),
        jax.ShapeDtypeStruct((N_TOK, ATT_WIDTH), F32),
        jax.ShapeDtypeStruct((N_TOK, ATT_WIDTH), F32),
        jax.ShapeDtypeStruct((N_TOK, N_HEADS), F32),
        jax.ShapeDtypeStruct((N_TOK, CONV_CH), F32),
    )
    out_specs = (
        pl.BlockSpec((1, 4, TM, LANES), lambda i: (i, 0, 0, 0)),
        pl.BlockSpec((TM, LANES), row),
        pl.BlockSpec((1, 4, TM, LANES), lambda i: (i, 0, 0, 0)),
        pl.BlockSpec((TM, LANES), row),
        pl.BlockSpec((1, N_HEADS, VT_ROWS, TM), lambda i: (i, 0, 0, 0)),
        pl.BlockSpec((TM, ATT_WIDTH), row),
        pl.BlockSpec((TM, ATT_WIDTH), row),
        pl.BlockSpec((TM, N_HEADS), row),
        pl.BlockSpec((TM, CONV_CH), row),
    )
    return pl.pallas_call(
        _inproj_kernel,
        grid=(NT,),
        in_specs=[pl.BlockSpec((TM, D_MODEL), row),
                  _const_spec((1, D_MODEL)), _const_spec((D_MODEL, IN_PAD)), _const_spec((1, LANES)),
                  _const_spec((1, ATT_WIDTH)), _const_spec((1, ATT_WIDTH)),
                  _const_spec((ATT_WIDTH, ATT_WIDTH)), _const_spec((TM, TM))],
        out_specs=out_specs,
        out_shape=out_shape,
        scratch_shapes=[pltpu.VMEM((8, LANES), F32)],
        compiler_params=_params(("arbitrary",)),
        name="inproj",
    )(h, gmix, w, bfp, gq, gk, bd, tri)


def _attn_kernel(q4_ref, augq_ref, k4_ref, augk_ref, vt_ref, gatt_ref, o_ref, qa_scr, m_scr, acc_scr, st_scr):
    qi = pl.program_id(1)
    lane = lax.broadcasted_iota(jnp.int32, (TM, LANES), 1)
    augq = augq_ref[...]
    zero = jnp.zeros((), BF16)
    for h in range(N_HEADS):
        qm = jnp.where((lane // HEAD_DIM) == (h % 2), q4_ref[0, h // 2], zero)
        am = jnp.where(((lane % 8) == h) & (lane < 48), augq, zero)
        qa_scr[h] = jnp.concatenate([qm, am], axis=1)
    m_scr[...] = jnp.full((N_HEADS, 1, TM), NEG, F32)
    acc_scr[...] = jnp.zeros((N_HEADS, VT_ROWS, TM), F32)

    key_pos = lax.broadcasted_iota(jnp.int32, (TM, TM), 0)
    qry_pos = lax.broadcasted_iota(jnp.int32, (TM, TM), 1)
    causal = key_pos <= qry_pos

    def scores(kb, slot):
        aug = augk_ref[kb]
        keep = causal | (kb < qi)
        for h in range(N_HEADS):
            ka = jnp.concatenate([k4_ref[kb, h // 2], aug], axis=1)
            st = lax.dot_general(ka, qa_scr[h], (((1,), (1,)), ((), ())),
                                 preferred_element_type=F32)
            st_scr[slot, h] = jnp.where(keep, st, NEG)

    def softmax_pv(kb, slot):
        for h in range(N_HEADS):
            m = m_scr[h]
            m_new = jnp.maximum(m, jnp.max(st_scr[slot, h], axis=0, keepdims=True))
            p = jnp.exp(st_scr[slot, h] - m_new).astype(BF16)
            pv = jnp.dot(vt_ref[kb, h], p, preferred_element_type=F32)
            acc_scr[h] = jnp.exp(m - m_new) * acc_scr[h] + pv
            m_scr[h] = m_new

    scores(0, 0)

    def body(kb, carry):
        scores(kb + 1, (kb + 1) % 2)
        softmax_pv(kb, kb % 2)
        return carry

    lax.fori_loop(0, qi, body, 0)
    softmax_pv(qi, qi % 2)
    acc = acc_scr[...]
    ot = acc[:, 0:HEAD_DIM, :] / acc[:, HEAD_DIM:HEAD_DIM + 1, :]
    o = ot.reshape(ATT_WIDTH, TM).T
    o_ref[...] = _rms(o, gatt_ref[...]).astype(BF16)


def _attn_prompt(q4, augq, k4, augk3, vt, gatt):
    seq_blk = lambda b, qi: (b, 0, 0, 0)
    once = pl.Buffered(1)
    return pl.pallas_call(
        _attn_kernel,
        grid=(BATCH, TILES_PER_SEQ),
        in_specs=[pl.BlockSpec((1, 4, TM, LANES), lambda b, qi: (b * TILES_PER_SEQ + qi, 0, 0, 0)),
                  pl.BlockSpec((TM, LANES), lambda b, qi: (b * TILES_PER_SEQ + qi, 0)),
                  pl.BlockSpec((TILES_PER_SEQ, 4, TM, LANES), seq_blk, pipeline_mode=once),
                  pl.BlockSpec((TILES_PER_SEQ, TM, LANES), lambda b, qi: (b, 0, 0), pipeline_mode=once),
                  pl.BlockSpec((TILES_PER_SEQ, N_HEADS, VT_ROWS, TM), seq_blk, pipeline_mode=once),
                  _const_spec((1, ATT_WIDTH))],
        out_specs=pl.BlockSpec((TM, ATT_WIDTH), lambda b, qi: (b * TILES_PER_SEQ + qi, 0)),
        out_shape=jax.ShapeDtypeStruct((N_PROMPT, ATT_WIDTH), BF16),
        scratch_shapes=[pltpu.VMEM((N_HEADS, TM, 2 * LANES), BF16),
                        pltpu.VMEM((N_HEADS, 1, TM), F32),
                        pltpu.VMEM((N_HEADS, VT_ROWS, TM), F32),
                        pltpu.VMEM((2, N_HEADS, TM, TM), F32)],
        compiler_params=_params(("arbitrary", "arbitrary")),
        name="attn_prompt",
    )(q4, augq, k4, augk3, vt, gatt)


def _suffix_kernel(x_ref, upper_ref, pages_ref, hi_ref, mid_ref, lo_ref):
    x = x_ref[0]
    pieces = _split3(x)
    upper = upper_ref[...]
    within = sum(jnp.dot(upper, p, preferred_element_type=F32) for p in pieces)
    page_tot = within[0:1, :] + x[0:1, :]
    tot8 = jnp.broadcast_to(page_tot, (8, N_PAGES * N_HEADS))
    later = sum(jnp.dot(p, pages_ref[...], preferred_element_type=F32) for p in _split3(tot8))
    d = within + later[0:1, :]
    hi, mid, lo = _split3(d)
    hi_ref[0] = hi
    mid_ref[0] = mid
    lo_ref[0] = lo


def _suffix(x, upper, pages):
    width = N_PAGES * N_HEADS
    blk = pl.BlockSpec((1, PAGE_SIZE, width), lambda b: (b, 0, 0))
    shp = jax.ShapeDtypeStruct((DEC_BATCH, PAGE_SIZE, width), BF16)
    return pl.pallas_call(
        _suffix_kernel,
        grid=(DEC_BATCH,),
        in_specs=[blk, _const_spec((PAGE_SIZE, PAGE_SIZE)), _const_spec((width, width))],
        out_specs=(blk, blk, blk),
        out_shape=(shp, shp, shp),
        compiler_params=_params(("arbitrary",)),
        name="suffix_logf",
    )(x, upper, pages)


def _attn_sample_kernel(pt_ref, *refs):
    del pt_ref
    kp = refs[:PAGES_PER_STEP]
    vp = refs[PAGES_PER_STEP:2 * PAGES_PER_STEP]
    (dp_ref, qe_ref, cb_ref, knew_ref, vnew_ref, ynew_ref, gatt_ref,
     o_ref, m_scr, l_scr, acc_scr) = refs[2 * PAGES_PER_STEP:]
    s = pl.program_id(1)
    qe = qe_ref[0]
    cb = cb_ref[0]

    def update(st, vmat, m, l, acc):
        m_new = jnp.maximum(m, jnp.max(st, axis=0, keepdims=True))
        alpha = jnp.exp(m - m_new)
        p = jnp.exp(st - m_new)
        l = alpha * l + jnp.sum(p, axis=0, keepdims=True)
        pv = jnp.dot(p.T.astype(BF16), vmat, preferred_element_type=F32)
        alpha_col = jnp.broadcast_to(alpha, (LANES, LANES)).T
        acc = acc * jnp.concatenate([alpha_col] * 4, axis=1) + pv
        return m_new, l, acc

    @pl.when(s == 0)
    def _():
        kn = jnp.concatenate([knew_ref[0].astype(BF16), jnp.zeros((LANES - DEC_SEQ, ATT_WIDTH), BF16)], axis=0)
        vn = jnp.concatenate([vnew_ref[0].astype(BF16), jnp.zeros((LANES - DEC_SEQ, ATT_WIDTH), BF16)], axis=0)
        st = jnp.dot(kn, qe[:ATT_WIDTH], preferred_element_type=F32)
        yn = jnp.concatenate([ynew_ref[0], jnp.zeros((LANES - DEC_SEQ, LANES), F32)], axis=0)
        st = st + cb - yn
        key = lax.broadcasted_iota(jnp.int32, (LANES, LANES), 0)
        col = lax.broadcasted_iota(jnp.int32, (LANES, LANES), 1)
        st = jnp.where((key < DEC_SEQ) & (key <= col % DEC_SEQ), st, NEG)
        m, l, acc = update(st, vn, jnp.full((1, LANES), NEG, F32), jnp.zeros((1, LANES), F32),
                           jnp.zeros((LANES, ATT_WIDTH), F32))
        m_scr[...] = m
        l_scr[...] = l
        acc_scr[...] = acc

    kmat = jnp.concatenate([r[0, 0].astype(BF16) for r in kp], axis=0)
    vmat = jnp.concatenate([r[0, 0].astype(BF16) for r in vp], axis=0)
    lhs = jnp.concatenate([kmat, dp_ref[0]], axis=1)
    st = jnp.dot(lhs, qe, preferred_element_type=F32) + cb
    m, l, acc = update(st, vmat, m_scr[...], l_scr[...], acc_scr[...])
    m_scr[...] = m
    l_scr[...] = l
    acc_scr[...] = acc

    @pl.when(s == STEPS_PER_SEQ - 1)
    def _():
        linv_col = jnp.broadcast_to(1.0 / l, (LANES, LANES)).T
        o = acc * jnp.concatenate([linv_col] * 4, axis=1)
        lane = lax.broadcasted_iota(jnp.int32, (DEC_SEQ, ATT_WIDTH), 1)
        att = jnp.zeros((DEC_SEQ, ATT_WIDTH), F32)
        for h in range(N_HEADS):
            att = att + jnp.where(lane // HEAD_DIM == h, o[h * DEC_SEQ:(h + 1) * DEC_SEQ], 0.0)
        o_ref[...] = _rms(att, gatt_ref[...])


def _attn_sample(layer, page_table, cache_k4, cache_v4, dp, qe, cb, knew, vnew, ynew, gatt):
    def page_spec(j):
        return pl.BlockSpec((1, 1, PAGE_SIZE, ATT_WIDTH),
                            lambda b, s, pt: (layer, pt[b, s * PAGES_PER_STEP + j], 0, 0))

    per_seq = lambda shape: pl.BlockSpec((1,) + shape, lambda b, s, pt: (b, 0, 0))
    grid_spec = pltpu.PrefetchScalarGridSpec(
        num_scalar_prefetch=1,
        grid=(DEC_BATCH, STEPS_PER_SEQ),
        in_specs=([page_spec(j) for j in range(PAGES_PER_STEP)]
                  + [page_spec(j) for j in range(PAGES_PER_STEP)]
                  + [pl.BlockSpec((1, KEYS_PER_STEP, LANES), lambda b, s, pt: (b, s, 0)),
                     per_seq((ATT_WIDTH + LANES, LANES)), per_seq((1, LANES)),
                     per_seq((DEC_SEQ, ATT_WIDTH)), per_seq((DEC_SEQ, ATT_WIDTH)),
                     per_seq((DEC_SEQ, LANES)),
                     pl.BlockSpec((1, ATT_WIDTH), lambda b, s, pt: (0, 0))]),
        out_specs=pl.BlockSpec((DEC_SEQ, ATT_WIDTH), lambda b, s, pt: (b, 0)),
        scratch_shapes=[pltpu.VMEM((1, LANES), F32), pltpu.VMEM((1, LANES), F32),
                        pltpu.VMEM((LANES, ATT_WIDTH), F32)],
    )
    return pl.pallas_call(
        _attn_sample_kernel,
        grid_spec=grid_spec,
        out_shape=jax.ShapeDtypeStruct((N_SAMPLE, ATT_WIDTH), F32),
        compiler_params=_params(("arbitrary", "arbitrary")),
        name="attn_sample",
    )(page_table, *([cache_k4] * PAGES_PER_STEP), *([cache_v4] * PAGES_PER_STEP),
      dp, qe, cb, knew, vnew, ynew, gatt)


def _conv_kernel(u_ref, prev_ref, state_ref, w_ref, bdw_ref, gln_ref, bln_ref, gcv_ref,
                 o_ref, xs_scr, xs3_scr, y_scr):
    t = pl.program_id(0)

    @pl.when(t < NT_PROMPT)
    def _():
        first = (t % TILES_PER_SEQ) == 0
        xs_scr[0:HALO, :] = jnp.where(first, 0.0, prev_ref[...])
        xs_scr[HALO:, :] = u_ref[...]
        acc = jnp.zeros((TM, CONV_CH), F32)
        for w in range(CONV_WIDTH):
            off = w + HALO - (CONV_WIDTH - 1)
            acc = acc + xs_scr[off:off + TM, :] * w_ref[w:w + 1, :]
        y_scr[...] = acc

    @pl.when(t == NT_PROMPT)
    def _():
        xs3_scr[:, 0:HALO, :] = state_ref[...]
        xs3_scr[:, HALO:, :] = u_ref[...].reshape(DEC_BATCH, DEC_SEQ, CONV_CH)
        acc = jnp.zeros((DEC_BATCH, DEC_SEQ, CONV_CH), F32)
        for w in range(CONV_WIDTH):
            off = w + HALO - (CONV_WIDTH - 1)
            acc = acc + xs3_scr[:, off:off + DEC_SEQ, :] * w_ref[w:w + 1, :]
        y_scr[...] = acc.reshape(TM, CONV_CH)

    y = y_scr[...] + bdw_ref[...]
    yc = y - jnp.mean(y, axis=-1, keepdims=True)
    yn = yc * lax.rsqrt(jnp.mean(yc * yc, axis=-1, keepdims=True) + EPS) * gln_ref[...] + bln_ref[...]
    o_ref[...] = _rms(_silu(yn), gcv_ref[...]).astype(BF16)


def _conv(u, state_pad, w, bdw, gln, bln, gcv):
    halo_blocks = TM // HALO
    return pl.pallas_call(
        _conv_kernel,
        grid=(NT,),
        in_specs=[pl.BlockSpec((TM, CONV_CH), lambda t: (t, 0)),
                  pl.BlockSpec((HALO, CONV_CH), lambda t: (jnp.maximum(t * halo_blocks - 1, 0), 0)),
                  _const_spec((DEC_BATCH, HALO, CONV_CH)),
                  _const_spec((HALO, CONV_CH)),
                  _const_spec((1, CONV_CH)), _const_spec((1, CONV_CH)),
                  _const_spec((1, CONV_CH)), _const_spec((1, CONV_CH))],
        out_specs=pl.BlockSpec((TM, CONV_CH), lambda t: (t, 0)),
        out_shape=jax.ShapeDtypeStruct((N_TOK, CONV_CH), BF16),
        scratch_shapes=[pltpu.VMEM((TM + HALO, CONV_CH), F32),
                        pltpu.VMEM((DEC_BATCH, HALO + DEC_SEQ, CONV_CH), F32),
                        pltpu.VMEM((TM, CONV_CH), F32)],
        compiler_params=_params(("arbitrary",)),
        name="conv",
    )(u, u, state_pad, w, bdw, gln, bln, gcv)


def _top2_gates(logits):
    lane = lax.broadcasted_iota(jnp.int32, logits.shape, 1)
    m1 = jnp.max(logits, axis=-1, keepdims=True)
    i1 = jnp.min(jnp.where(logits == m1, lane, LANES), axis=-1, keepdims=True)
    rest = jnp.where(lane == i1, NEG, logits)
    m2 = jnp.max(rest, axis=-1, keepdims=True)
    i2 = jnp.min(jnp.where(rest == m2, lane, LANES), axis=-1, keepdims=True)
    e = jnp.exp(m2 - m1)
    w1 = 1.0 / (1.0 + e)
    w2 = e / (1.0 + e)
    gates = jnp.where(lane == i1, w1, 0.0) + jnp.where(lane == i2, w2, 0.0)
    chosen = jnp.where((lane == i1) | (lane == i2), 1.0, 0.0)
    return gates, chosen


def _outproj_kernel(*refs, with_router):
    if with_router:
        (h_ref, attp_ref, atts_ref, cn_ref, wa_ref, wc_ref, gffn_ref, wr_ref,
         h1_ref, n_ref, gates_ref, chosen_ref, chosen_t_ref) = refs
    else:
        h_ref, attp_ref, atts_ref, cn_ref, wa_ref, wc_ref, gffn_ref, h1_ref, n_ref = refs
    t = pl.program_id(0)
    att = jnp.where(t == NT_PROMPT, atts_ref[...].astype(BF16), attp_ref[...])
    h1 = (h_ref[...]
          + jnp.dot(att, wa_ref[...], preferred_element_type=F32)
          + jnp.dot(cn_ref[...], wc_ref[...], preferred_element_type=F32))
    h1_ref[...] = h1
    n = _rms(h1, gffn_ref[...])
    n_ref[...] = n.astype(BF16)
    if with_router:
        logits = jnp.dot(n.astype(BF16), wr_ref[...], preferred_element_type=F32)
        lane = lax.broadcasted_iota(jnp.int32, logits.shape, 1)
        gates, chosen = _top2_gates(jnp.where(lane < N_EXPERTS, logits, NEG))
        gates_ref[...] = gates
        chosen_ref[...] = chosen
        chosen_t_ref[...] = chosen.T[:N_EXPERTS]


def _outproj(h, attp, atts, cn, wa, wc, gffn, router=None):
    row = lambda t: (t, 0)
    in_specs = [pl.BlockSpec((TM, D_MODEL), row),
                pl.BlockSpec((TM, ATT_WIDTH), lambda t: (jnp.minimum(t, NT_PROMPT - 1), 0)),
                _const_spec((N_SAMPLE, ATT_WIDTH)),
                pl.BlockSpec((TM, CONV_CH), row),
                _const_spec((ATT_WIDTH, D_MODEL)), _const_spec((CONV_CH, D_MODEL)),
                _const_spec((1, D_MODEL))]
    out_specs = [pl.BlockSpec((TM, D_MODEL), row), pl.BlockSpec((TM, D_MODEL), row)]
    out_shape = [jax.ShapeDtypeStruct((N_TOK, D_MODEL), F32), jax.ShapeDtypeStruct((N_TOK, D_MODEL), BF16)]
    args = [h, attp, atts, cn, wa, wc, gffn]
    if router is not None:
        in_specs += [_const_spec((D_MODEL, LANES))]
        out_specs += [pl.BlockSpec((TM, LANES), row), pl.BlockSpec((TM, LANES), row),
                      pl.BlockSpec((N_EXPERTS, TM), lambda t: (0, t))]
        out_shape += [jax.ShapeDtypeStruct((N_TOK, LANES), F32), jax.ShapeDtypeStruct((N_TOK, LANES), F32),
                      jax.ShapeDtypeStruct((N_EXPERTS, N_TOK), F32)]
        args.append(router)
    return pl.pallas_call(
        functools.partial(_outproj_kernel, with_router=router is not None),
        grid=(NT,),
        in_specs=in_specs,
        out_specs=tuple(out_specs),
        out_shape=tuple(out_shape),
        compiler_params=_params(("arbitrary",)),
        name="outproj",
    )(*args)


def _ple(h2, p_ref, gple_ref, wproj_ref, wgate_ref):
    gate = _sigmoid(jnp.dot(_rms(h2, gple_ref[...]).astype(BF16), wgate_ref[...],
                            preferred_element_type=F32))
    proj = jnp.dot(p_ref[...].astype(BF16), wproj_ref[...], preferred_element_type=F32)
    return h2 + proj * gate


def _ffn_ple_kernel(h1_ref, n_ref, wg_ref, wu_ref, wd_ref, p_ref, gple_ref, wproj_ref, wgate_ref, o_ref):
    n = n_ref[...]
    g = jnp.dot(n, wg_ref[...], preferred_element_type=F32)
    u = jnp.dot(n, wu_ref[...], preferred_element_type=F32)
    a = (_silu(g) * u).astype(BF16)
    h2 = h1_ref[...] + jnp.dot(a, wd_ref[...], preferred_element_type=F32)
    o_ref[...] = _ple(h2, p_ref, gple_ref, wproj_ref, wgate_ref)


def _ffn_ple(h1, n, wg, wu, wd, p, gple, wproj, wgate):
    row = lambda t: (t, 0)
    return pl.pallas_call(
        _ffn_ple_kernel,
        grid=(NT,),
        in_specs=[pl.BlockSpec((TM, D_MODEL), row), pl.BlockSpec((TM, D_MODEL), row),
                  _const_spec((D_MODEL, D_FF)), _const_spec((D_MODEL, D_FF)), _const_spec((D_FF, D_MODEL)),
                  pl.BlockSpec((TM, PLE_DIM), row), _const_spec((1, D_MODEL)),
                  _const_spec((PLE_DIM, D_MODEL)), _const_spec((D_MODEL, D_MODEL))],
        out_specs=pl.BlockSpec((TM, D_MODEL), row),
        out_shape=jax.ShapeDtypeStruct((N_TOK, D_MODEL), F32),
        compiler_params=_params(("arbitrary",)),
        name="ffn_ple",
    )(h1, n, wg, wu, wd, p, gple, wproj, wgate)


def _moe_kernel(cnt_ref, h1_ref, n_ref, gates_ref, chosen_t_ref, before_ref, wg_ref, wu_ref, wd_ref,
                o_ref, acc_scr, slot_scr, xe_scr, ye_scr):
    t = pl.program_id(0)
    e = pl.program_id(1)
    c = pl.program_id(2)
    last_c = pl.num_programs(2) - 1
    n_chunks = (cnt_ref[t, e] + MOE_CH - 1) // MOE_CH

    @pl.when((e == 0) & (c == 0))
    def _():
        acc_scr[...] = h1_ref[...]
        chosen_t = chosen_t_ref[...]
        before = jnp.dot(chosen_t.astype(BF16), before_ref[...], preferred_element_type=F32)
        slot_scr[...] = jnp.where(chosen_t > 0.5, before, -1.0)

    def selection(j):
        slot = slot_scr[pl.ds(e, 1), :]
        row = (lax.broadcasted_iota(jnp.int32, (MOE_CH, MOE_TM), 0) + j * MOE_CH).astype(F32)
        return jnp.where(slot == row, 1.0, 0.0).astype(BF16)

    @pl.when(c == 0)
    def _():
        def gather(j, carry):
            r0 = pl.multiple_of(j * MOE_CH, MOE_CH)
            xe_scr[pl.ds(r0, MOE_CH), :] = jnp.dot(
                selection(j), n_ref[...], preferred_element_type=F32).astype(BF16)
            return carry
        lax.fori_loop(0, n_chunks, gather, 0)

    def expert(j, carry):
        r0 = pl.multiple_of(j * MOE_CH, MOE_CH)
        x = xe_scr[pl.ds(r0, MOE_CH), :]
        g = jnp.dot(x, wg_ref[0], preferred_element_type=F32)
        u = jnp.dot(x, wu_ref[0], preferred_element_type=F32)
        y = jnp.dot((_silu(g) * u).astype(BF16), wd_ref[0], preferred_element_type=F32)

        @pl.when(c == 0)
        def _():
            ye_scr[pl.ds(r0, MOE_CH), :] = y

        @pl.when(c > 0)
        def _():
            ye_scr[pl.ds(r0, MOE_CH), :] += y
        return carry

    lax.fori_loop(0, n_chunks, expert, 0)

    @pl.when(c == last_c)
    def _():
        lane = lax.broadcasted_iota(jnp.int32, (MOE_TM, LANES), 1)
        gate = jnp.sum(jnp.where(lane == e, gates_ref[...], 0.0), axis=-1, keepdims=True)

        def scatter(j, carry):
            r0 = pl.multiple_of(j * MOE_CH, MOE_CH)
            y = ye_scr[pl.ds(r0, MOE_CH), :]
            y_hi = y.astype(BF16)
            y_lo = (y - y_hi.astype(F32)).astype(BF16)
            sel = selection(j)
            back = lax.dot_general(jnp.concatenate([sel, sel], axis=0), jnp.concatenate([y_hi, y_lo], axis=0),
                                   (((0,), (0,)), ((), ())), preferred_element_type=F32)
            acc_scr[...] += gate * back
            return carry
        lax.fori_loop(0, n_chunks, scatter, 0)

    @pl.when((e == N_EXPERTS - 1) & (c == last_c))
    def _():
        o_ref[...] = acc_scr[...]


def _moe(counts, h1, n, gates, chosen_t, before, wg, wu, wd):
    row = lambda t, e, c, cnt: (t, 0)
    once = pl.Buffered(1)
    grid_spec = pltpu.PrefetchScalarGridSpec(
        num_scalar_prefetch=1,
        grid=(N_TOK // MOE_TM, N_EXPERTS, D_FF_EXPERT // MOE_TF),
        in_specs=[pl.BlockSpec((MOE_TM, D_MODEL), row, pipeline_mode=once),
                  pl.BlockSpec((MOE_TM, D_MODEL), row, pipeline_mode=once),
                  pl.BlockSpec((MOE_TM, LANES), row, pipeline_mode=once),
                  pl.BlockSpec((N_EXPERTS, MOE_TM), lambda t, e, c, cnt: (0, t), pipeline_mode=once),
                  pl.BlockSpec((MOE_TM, MOE_TM), lambda t, e, c, cnt: (0, 0), pipeline_mode=once),
                  pl.BlockSpec((1, D_MODEL, MOE_TF), lambda t, e, c, cnt: (e, 0, c)),
                  pl.BlockSpec((1, D_MODEL, MOE_TF), lambda t, e, c, cnt: (e, 0, c)),
                  pl.BlockSpec((1, MOE_TF, D_MODEL), lambda t, e, c, cnt: (e, c, 0))],
        out_specs=pl.BlockSpec((MOE_TM, D_MODEL), row),
        scratch_shapes=[pltpu.VMEM((MOE_TM, D_MODEL), F32),
                        pltpu.VMEM((N_EXPERTS, MOE_TM), F32),
                        pltpu.VMEM((MOE_TM, D_MODEL), BF16),
                        pltpu.VMEM((MOE_TM, D_MODEL), F32)],
    )
    return pl.pallas_call(
        _moe_kernel,
        grid_spec=grid_spec,
        out_shape=jax.ShapeDtypeStruct((N_TOK, D_MODEL), F32),
        compiler_params=_params(("arbitrary", "arbitrary", "arbitrary")),
        name="moe",
    )(counts, h1, n, gates, chosen_t, before, wg, wu, wd)


def _ple_kernel(h2_ref, p_ref, gple_ref, wproj_ref, wgate_ref, o_ref):
    o_ref[...] = _ple(h2_ref[...], p_ref, gple_ref, wproj_ref, wgate_ref)


def _ple_call(h2, p, gple, wproj, wgate):
    row = lambda t: (t, 0)
    return pl.pallas_call(
        _ple_kernel,
        grid=(NT,),
        in_specs=[pl.BlockSpec((TM, D_MODEL), row), pl.BlockSpec((TM, PLE_DIM), row),
                  _const_spec((1, D_MODEL)), _const_spec((PLE_DIM, D_MODEL)),
                  _const_spec((D_MODEL, D_MODEL))],
        out_specs=pl.BlockSpec((TM, D_MODEL), row),
        out_shape=jax.ShapeDtypeStruct((N_TOK, D_MODEL), F32),
        compiler_params=_params(("arbitrary",)),
        name="ple",
    )(h2, p, gple, wproj, wgate)


def _newkey_kernel(logf_ref, expand_ref, y_ref, cb_ref):
    lf = logf_ref[0]
    row = lax.broadcasted_iota(jnp.int32, (DEC_SEQ, LANES), 0)
    cn = jnp.zeros((DEC_SEQ, LANES), F32)
    for r in range(DEC_SEQ):
        cn = cn + jnp.where(row >= r, lf[r:r + 1, :], 0.0)
    y = sum(jnp.dot(p, expand_ref[...], preferred_element_type=F32) for p in _split3(cn))
    y_ref[0] = y
    lane = lax.broadcasted_iota(jnp.int32, (DEC_SEQ, LANES), 1)
    cb_ref[0] = jnp.sum(jnp.where(lane % DEC_SEQ == row, y, 0.0), axis=0, keepdims=True)


def _newkey(logf_pad, expand):
    blk = lambda shape: pl.BlockSpec((1,) + shape, lambda b: (b, 0, 0))
    return pl.pallas_call(
        _newkey_kernel,
        grid=(DEC_BATCH,),
        in_specs=[blk((DEC_SEQ, LANES)), _const_spec((LANES, LANES))],
        out_specs=(blk((DEC_SEQ, LANES)), blk((1, LANES))),
        out_shape=(jax.ShapeDtypeStruct((DEC_BATCH, DEC_SEQ, LANES), F32),
                   jax.ShapeDtypeStruct((DEC_BATCH, 1, LANES), F32)),
        compiler_params=_params(("arbitrary",)),
        name="newkey_bias",
    )(logf_pad, expand)


def kernel(x_prompt, x_sample, cache_k, cache_v, cache_logf, state_conv, page_table, p_prompt, p_sample, g_mix, w_in, b_forget, g_q, g_k, w_dw, b_dw, g_conv_ln, b_conv_ln, g_out_att, g_out_conv, w_out, g_ffn, w_ff_gate, w_ff_up, w_ff_down, w_router, w_ex_gate, w_ex_up, w_ex_down, g_ple, w_ple_proj, w_ple_gate):
    n_pool = cache_k.shape[1]
    h = jnp.concatenate([x_prompt.reshape(N_PROMPT, D_MODEL), x_sample.reshape(N_SAMPLE, D_MODEL)], axis=0)
    cache_k4 = cache_k.reshape(DEPTH, n_pool, PAGE_SIZE, ATT_WIDTH)
    cache_v4 = cache_v.reshape(DEPTH, n_pool, PAGE_SIZE, ATT_WIDTH)

    idx = jnp.arange(ATT_WIDTH)
    bd = jnp.where(idx[:, None] // HEAD_DIM == idx[None, :] // HEAD_DIM, 1.0 / HEAD_DIM, 0.0).astype(BF16)
    r = jnp.arange(TM)
    tri = (r[None, :] <= r[:, None]).astype(BF16)
    rm = jnp.arange(MOE_TM)
    before = (rm[:, None] < rm[None, :]).astype(BF16)
    rp = jnp.arange(PAGE_SIZE)
    upper = (rp[None, :] > rp[:, None]).astype(BF16)
    cw = jnp.arange(N_PAGES * N_HEADS)
    pages = ((cw[:, None] % N_HEADS == cw[None, :] % N_HEADS)
             & (cw[:, None] // N_HEADS > cw[None, :] // N_HEADS)).astype(BF16)
    ln = jnp.arange(LANES)
    col_head = jnp.where(ln < N_HEADS * DEC_SEQ, ln // DEC_SEQ, -1)
    expand = ((ln[:, None] == col_head[None, :]) & (ln[:, None] < N_HEADS)).astype(BF16)
    e_rows = jnp.where(ln < 3 * N_HEADS, ln % N_HEADS, -2)
    e_part = (e_rows[:, None] == col_head[None, :]).astype(BF16)
    head_of_row = jnp.arange(ATT_WIDTH) // HEAD_DIM
    q_mask = (head_of_row[:, None] == col_head[None, :])

    states = []
    for i in range(DEPTH):
        w_pad = jnp.pad(w_in[i], ((0, 0), (0, IN_PAD - IN_WIDTH))).astype(BF16)
        bfp = jnp.pad(b_forget[i], (0, LANES - N_HEADS)).reshape(1, LANES)
        gq = (jnp.tile(g_q[i], N_HEADS) * ATT_SCALE).reshape(1, ATT_WIDTH)
        gk = jnp.tile(g_k[i], N_HEADS).reshape(1, ATT_WIDTH)
        q4, augq, k4, augk, vt, k, v, logf, u = _inproj(
            h, g_mix[i].reshape(1, D_MODEL), w_pad, bfp, gq, gk, bd, tri)
        gatt = g_out_att[i].reshape(1, ATT_WIDTH)

        attp = _attn_prompt(q4, augq, k4, augk.reshape(NT, TM, LANES), vt, gatt)

        f_past = cache_logf[i][page_table]
        x_suf = f_past.transpose(0, 2, 1, 3).reshape(DEC_BATCH, PAGE_SIZE, N_PAGES * N_HEADS)
        d_pieces = _suffix(x_suf, upper, pages)
        dp = jnp.concatenate(
            [d.reshape(DEC_BATCH, PAGE_SIZE, N_PAGES, N_HEADS).transpose(0, 2, 1, 3)
              .reshape(DEC_BATCH, PAST_LEN, N_HEADS) for d in d_pieces], axis=-1)
        dp = jnp.pad(dp, ((0, 0), (0, 0), (0, LANES - 3 * N_HEADS)))
        q_s = q4[NT_PROMPT].transpose(1, 0, 2).reshape(DEC_BATCH, DEC_SEQ, ATT_WIDTH)
        q_cols = jnp.tile(q_s.transpose(0, 2, 1), (1, 1, N_HEADS))
        q_cols = jnp.pad(q_cols, ((0, 0), (0, 0), (0, LANES - N_HEADS * DEC_SEQ)))
        qblk = jnp.where(q_mask[None], q_cols, jnp.zeros((), BF16))
        qe = jnp.concatenate([qblk, jnp.broadcast_to(e_part, (DEC_BATCH, LANES, LANES))], axis=1)
        logf_s = jnp.pad(logf[N_PROMPT:], ((0, 0), (0, LANES - N_HEADS))).reshape(DEC_BATCH, DEC_SEQ, LANES)
        ynew, cb = _newkey(logf_s, expand)
        k_s = k[N_PROMPT:].reshape(DEC_BATCH, DEC_SEQ, ATT_WIDTH)
        v_s = v[N_PROMPT:].reshape(DEC_BATCH, DEC_SEQ, ATT_WIDTH)
        atts = _attn_sample(i, page_table, cache_k4, cache_v4, dp, qe, cb, k_s, v_s, ynew, gatt)

        state_pad = jnp.pad(state_conv[i], ((0, 0), (HALO - (CONV_WIDTH - 1), 0), (0, 0)))
        w_dw_pad = jnp.pad(w_dw[i], ((0, HALO - CONV_WIDTH), (0, 0)))
        cn = _conv(u, state_pad, w_dw_pad, b_dw[i].reshape(1, CONV_CH), g_conv_ln[i].reshape(1, CONV_CH),
                   b_conv_ln[i].reshape(1, CONV_CH), g_out_conv[i].reshape(1, CONV_CH))

        wo = w_out[i].astype(BF16)
        p_all = jnp.concatenate([p_prompt[i].reshape(N_PROMPT, PLE_DIM), p_sample[i].reshape(N_SAMPLE, PLE_DIM)], axis=0)
        gple = g_ple[i].reshape(1, D_MODEL)
        wproj = w_ple_proj[i].astype(BF16)
        wgate = w_ple_gate[i].astype(BF16)
        gffn = g_ffn[i].reshape(1, D_MODEL)
        j = i // 2
        if i % 2 == 0:
            h1, n = _outproj(h, attp, atts, cn, wo[:ATT_WIDTH], wo[ATT_WIDTH:], gffn)
            h = _ffn_ple(h1, n, w_ff_gate[j].astype(BF16), w_ff_up[j].astype(BF16),
                         w_ff_down[j].astype(BF16), p_all, gple, wproj, wgate)
        else:
            wr = jnp.pad(w_router[j], ((0, 0), (0, LANES - N_EXPERTS))).astype(BF16)
            h1, n, gates, chosen, chosen_t = _outproj(h, attp, atts, cn, wo[:ATT_WIDTH], wo[ATT_WIDTH:], gffn,
                                                      router=wr)
            counts = jnp.sum(chosen[:, :N_EXPERTS].reshape(N_TOK // MOE_TM, MOE_TM, N_EXPERTS),
                             axis=1).astype(jnp.int32)
            h2 = _moe(counts, h1, n, gates, chosen_t, before, w_ex_gate[j].astype(BF16),
                      w_ex_up[j].astype(BF16), w_ex_down[j].astype(BF16))
            h = _ple_call(h2, p_all, gple, wproj, wgate)

        conv_p = u[:N_PROMPT].reshape(BATCH, SEQ, CONV_CH)[:, SEQ - (CONV_WIDTH - 1):]
        conv_s = jnp.concatenate([state_conv[i], u[N_PROMPT:].reshape(DEC_BATCH, DEC_SEQ, CONV_CH)],
                                 axis=1)[:, DEC_SEQ:]
        states.append((k, v, logf, conv_p, conv_s))

    def stack(idx, lo, hi, shape):
        return jnp.stack([s[idx][lo:hi].reshape(shape) for s in states])

    kv_p = (BATCH, SEQ, N_HEADS, HEAD_DIM)
    kv_s = (DEC_BATCH, DEC_SEQ, N_HEADS, HEAD_DIM)
    return (h[:N_PROMPT].reshape(BATCH, SEQ, D_MODEL),
            h[N_PROMPT:].reshape(DEC_BATCH, DEC_SEQ, D_MODEL),
            stack(0, 0, N_PROMPT, kv_p), stack(1, 0, N_PROMPT, kv_p),
            stack(2, 0, N_PROMPT, (BATCH, SEQ, N_HEADS)),
            jnp.stack([s[3] for s in states]),
            stack(0, N_PROMPT, N_TOK, kv_s), stack(1, N_PROMPT, N_TOK, kv_s),
            stack(2, N_PROMPT, N_TOK, (DEC_BATCH, DEC_SEQ, N_HEADS)),
            jnp.stack([s[4] for s in states]))
```

```python
import functools

import jax
import jax.numpy as jnp
from jax import lax
from jax.experimental import pallas as pl
from jax.experimental.pallas import tpu as pltpu

F32 = jnp.float32
BF16 = jnp.bfloat16

D_MODEL = 1024
BATCH = 2
SEQ = 8192
DEPTH = 4
DEC_BATCH = 32
DEC_SEQ = 8
PAST_LEN = 8192
PAGE_SIZE = 128
N_PAGES = PAST_LEN // PAGE_SIZE
N_HEADS = 8
HEAD_DIM = 64
ATT_WIDTH = N_HEADS * HEAD_DIM
CONV_CH = D_MODEL // 2
CONV_WIDTH = 31
IN_WIDTH = 3 * ATT_WIDTH + 2 * CONV_CH + N_HEADS
D_FF = 2816
N_EXPERTS = 8
D_FF_EXPERT = D_MODEL * 7 // 2
PLE_DIM = 256
EPS = 1e-6
ATT_SCALE = HEAD_DIM ** -0.5

LANES = 128
N_PROMPT = BATCH * SEQ
N_SAMPLE = DEC_BATCH * DEC_SEQ
N_TOK = N_PROMPT + N_SAMPLE
TM = 256
NT = N_TOK // TM
NT_PROMPT = N_PROMPT // TM
TILES_PER_SEQ = SEQ // TM
IN_PAD = 3 * ATT_WIDTH + 2 * CONV_CH + LANES
VT_ROWS = HEAD_DIM + 16
HALO = 32
PAGES_PER_STEP = 8
KEYS_PER_STEP = PAGES_PER_STEP * PAGE_SIZE
STEPS_PER_SEQ = N_PAGES // PAGES_PER_STEP
MOE_TM = 1280
MOE_TF = 896
MOE_CH = 128
NEG = -1e30
VMEM_LIMIT = 56 * 1024 * 1024


def _params(sem, vmem=VMEM_LIMIT):
    return pltpu.CompilerParams(dimension_semantics=sem, vmem_limit_bytes=vmem)


def _const_spec(shape):
    nd = len(shape)
    return pl.BlockSpec(shape, lambda *_: (0,) * nd, pipeline_mode=pl.Buffered(1))


def _split3(x):
    hi = x.astype(BF16)
    r = x - hi.astype(F32)
    mid = r.astype(BF16)
    lo = (r - mid.astype(F32)).astype(BF16)
    return hi, mid, lo


def _rms(x, g):
    return x * lax.rsqrt(jnp.mean(x * x, axis=-1, keepdims=True) + EPS) * g


def _sigmoid(x):
    return 1.0 / (1.0 + jnp.exp(-x))


def _silu(x):
    return x * _sigmoid(x)


def _log_sigmoid(x):
    return jnp.minimum(x, 0.0) - jnp.log1p(jnp.exp(-jnp.abs(x)))


def _inproj_kernel(h_ref, gmix_ref, w_ref, bf_ref, gq_ref, gk_ref, bd_ref, tri_ref,
                   q4_ref, augq_ref, k4_ref, augk_ref, vt_ref, k_ref, v_ref, logf_ref, u_ref,
                   carry_ref):
    i = pl.program_id(0)
    xn = _rms(h_ref[...], gmix_ref[...]).astype(BF16)

    def proj(c0, c1):
        return jnp.dot(xn, w_ref[:, c0:c1], preferred_element_type=F32)

    def head_norm(z, g):
        ms = jnp.dot((z * z).astype(BF16), bd_ref[...], preferred_element_type=F32)
        return z * lax.rsqrt(ms + EPS) * g

    q = head_norm(proj(0, 512), gq_ref[...])
    k = head_norm(proj(512, 1024), gk_ref[...])
    v = proj(1024, 1536)
    ua = proj(1536, 2048)
    ug = proj(2048, 2560)
    zf = proj(2560, IN_PAD)

    lane = lax.broadcasted_iota(jnp.int32, (TM, LANES), 1)
    logf = jnp.where(lane < N_HEADS, _log_sigmoid(zf + bf_ref[...]), 0.0)

    k_ref[...] = k
    v_ref[...] = v
    u_ref[...] = ua * _sigmoid(ug)
    logf_ref[...] = logf[:, :N_HEADS]

    qb = q.astype(BF16)
    kb = k.astype(BF16)
    for j in range(4):
        q4_ref[0, j] = qb[:, j * LANES:(j + 1) * LANES]
        k4_ref[0, j] = kb[:, j * LANES:(j + 1) * LANES]
    vt_ref[0, :, 0:HEAD_DIM, :] = v.T.astype(BF16).reshape(N_HEADS, HEAD_DIM, TM)
    vt_ref[0, :, HEAD_DIM:VT_ROWS, :] = jnp.ones((N_HEADS, VT_ROWS - HEAD_DIM, TM), BF16)

    hi, mid, lo = _split3(logf)
    tri = tri_ref[...]
    c = (jnp.dot(tri, hi, preferred_element_type=F32)
         + jnp.dot(tri, mid, preferred_element_type=F32)
         + jnp.dot(tri, lo, preferred_element_type=F32))
    carry = jnp.where(i % TILES_PER_SEQ == 0, 0.0, carry_ref[0:1, :])
    c = c + carry
    carry_ref[...] = jnp.broadcast_to(c[TM - 1:TM, :], (8, LANES))

    chi, cmid, clo = (p.astype(F32) for p in _split3(c))
    ones_q = jnp.where((lane >= 24) & (lane < 48), 1.0, 0.0)
    ones_k = jnp.where(lane < 24, 1.0, 0.0)
    augq = chi + pltpu.roll(cmid, 8, 1) + pltpu.roll(clo, 16, 1) + ones_q
    augk = ones_k - pltpu.roll(chi, 24, 1) - pltpu.roll(cmid, 32, 1) - pltpu.roll(clo, 40, 1)
    augq_ref[...] = augq.astype(BF16)
    augk_ref[...] = augk.astype(BF16)


def _inproj(h, gmix, w, bfp, gq, gk, bd, tri):
    row = lambda i: (i, 0)
    out_shape = (
        jax.ShapeDtypeStruct((NT, 4, TM, LANES), BF16),
        jax.ShapeDtypeStruct((N_TOK, LANES), BF16),
        jax.ShapeDtypeStruct((NT, 4, TM, LANES), BF16),
        jax.ShapeDtypeStruct((N_TOK, LANES), BF16),
        jax.ShapeDtypeStruct((NT, N_HEADS, VT_ROWS, TM), BF16),
        jax.ShapeDtypeStruct((N_TOK, ATT_WIDTH), F32),
        jax.ShapeDtypeStruct((N_TOK, ATT_WIDTH), F32),
        jax.ShapeDtypeStruct((N_TOK, N_HEADS), F32),
        jax.ShapeDtypeStruct((N_TOK, CONV_CH), F32),
    )
    out_specs = (
        pl.BlockSpec((1, 4, TM, LANES), lambda i: (i, 0, 0, 0)),
        pl.BlockSpec((TM, LANES), row),
        pl.BlockSpec((1, 4, TM, LANES), lambda i: (i, 0, 0, 0)),
        pl.BlockSpec((TM, LANES), row),
        pl.BlockSpec((1, N_HEADS, VT_ROWS, TM), lambda i: (i, 0, 0, 0)),
        pl.BlockSpec((TM, ATT_WIDTH), row),
        pl.BlockSpec((TM, ATT_WIDTH), row),
        pl.BlockSpec((TM, N_HEADS), row),
        pl.BlockSpec((TM, CONV_CH), row),
    )
    return pl.pallas_call(
        _inproj_kernel,
        grid=(NT,),
        in_specs=[pl.BlockSpec((TM, D_MODEL), row),
                  _const_spec((1, D_MODEL)), _const_spec((D_MODEL, IN_PAD)), _const_spec((1, LANES)),
                  _const_spec((1, ATT_WIDTH)), _const_spec((1, ATT_WIDTH)),
                  _const_spec((ATT_WIDTH, ATT_WIDTH)), _const_spec((TM, TM))],
        out_specs=out_specs,
        out_shape=out_shape,
        scratch_shapes=[pltpu.VMEM((8, LANES), F32)],
        compiler_params=_params(("arbitrary",)),
        name="inproj",
    )(h, gmix, w, bfp, gq, gk, bd, tri)


def _attn_kernel(q4_ref, augq_ref, k4_ref, augk_ref, vt_ref, gatt_ref, o_ref, qa_scr, m_scr, acc_scr, st_scr):
    qi = pl.program_id(1)
    lane = lax.broadcasted_iota(jnp.int32, (TM, LANES), 1)
    augq = augq_ref[...]
    zero = jnp.zeros((), BF16)
    for h in range(N_HEADS):
        qm = jnp.where((lane // HEAD_DIM) == (h % 2), q4_ref[0, h // 2], zero)
        am = jnp.where(((lane % 8) == h) & (lane < 48), augq, zero)
        qa_scr[h] = jnp.concatenate([qm, am], axis=1)
    m_scr[...] = jnp.full((N_HEADS, 1, TM), NEG, F32)
    acc_scr[...] = jnp.zeros((N_HEADS, VT_ROWS, TM), F32)

    key_pos = lax.broadcasted_iota(jnp.int32, (TM, TM), 0)
    qry_pos = lax.broadcasted_iota(jnp.int32, (TM, TM), 1)
    causal = key_pos <= qry_pos

    def scores(kb, slot):
        aug = augk_ref[kb]
        keep = causal | (kb < qi)
        for h in range(N_HEADS):
            ka = jnp.concatenate([k4_ref[kb, h // 2], aug], axis=1)
            st = lax.dot_general(ka, qa_scr[h], (((1,), (1,)), ((), ())),
                                 preferred_element_type=F32)
            st_scr[slot, h] = jnp.where(keep, st, NEG)

    def softmax_pv(kb, slot):
        for h in range(N_HEADS):
            m = m_scr[h]
            m_new = jnp.maximum(m, jnp.max(st_scr[slot, h], axis=0, keepdims=True))
            p = jnp.exp(st_scr[slot, h] - m_new).astype(BF16)
            pv = jnp.dot(vt_ref[kb, h], p, preferred_element_type=F32)
            acc_scr[h] = jnp.exp(m - m_new) * acc_scr[h] + pv
            m_scr[h] = m_new

    scores(0, 0)

    def body(kk, carry):
        kb = 2 * kk
        scores(kb + 1, 1)
        softmax_pv(kb, 0)
        scores(kb + 2, 0)
        softmax_pv(kb + 1, 1)
        return carry

    lax.fori_loop(0, qi // 2, body, 0)

    @pl.when(qi % 2 == 0)
    def _():
        softmax_pv(qi, 0)

    @pl.when(qi % 2 == 1)
    def _():
        scores(qi, 1)
        softmax_pv(qi - 1, 0)
        softmax_pv(qi, 1)

    acc = acc_scr[...]
    ot = acc[:, 0:HEAD_DIM, :] / acc[:, HEAD_DIM:HEAD_DIM + 1, :]
    o = ot.reshape(ATT_WIDTH, TM).T
    o_ref[...] = _rms(o, gatt_ref[...]).astype(BF16)


def _attn_prompt(q4, augq, k4, augk3, vt, gatt):
    seq_blk = lambda b, qi: (b, 0, 0, 0)
    once = pl.Buffered(1)
    return pl.pallas_call(
        _attn_kernel,
        grid=(BATCH, TILES_PER_SEQ),
        in_specs=[pl.BlockSpec((1, 4, TM, LANES), lambda b, qi: (b * TILES_PER_SEQ + qi, 0, 0, 0)),
                  pl.BlockSpec((TM, LANES), lambda b, qi: (b * TILES_PER_SEQ + qi, 0)),
                  pl.BlockSpec((TILES_PER_SEQ, 4, TM, LANES), seq_blk, pipeline_mode=once),
                  pl.BlockSpec((TILES_PER_SEQ, TM, LANES), lambda b, qi: (b, 0, 0), pipeline_mode=once),
                  pl.BlockSpec((TILES_PER_SEQ, N_HEADS, VT_ROWS, TM), seq_blk, pipeline_mode=once),
                  _const_spec((1, ATT_WIDTH))],
        out_specs=pl.BlockSpec((TM, ATT_WIDTH), lambda b, qi: (b * TILES_PER_SEQ + qi, 0)),
        out_shape=jax.ShapeDtypeStruct((N_PROMPT, ATT_WIDTH), BF16),
        scratch_shapes=[pltpu.VMEM((N_HEADS, TM, 2 * LANES), BF16),
                        pltpu.VMEM((N_HEADS, 1, TM), F32),
                        pltpu.VMEM((N_HEADS, VT_ROWS, TM), F32),
                        pltpu.VMEM((2, N_HEADS, TM, TM), F32)],
        compiler_params=_params(("arbitrary", "arbitrary")),
        name="attn_prompt",
    )(q4, augq, k4, augk3, vt, gatt)


def _suffix_kernel(x_ref, upper_ref, pages_ref, d_ref):
    x = x_ref[0]
    pieces = _split3(x)
    upper = upper_ref[...]
    within = sum(jnp.dot(upper, p, preferred_element_type=F32) for p in pieces)
    page_tot = within[0:1, :] + x[0:1, :]
    tot8 = jnp.broadcast_to(page_tot, (8, N_PAGES * N_HEADS))
    later = sum(jnp.dot(p, pages_ref[...], preferred_element_type=F32) for p in _split3(tot8))
    d_ref[0] = within + later[0:1, :]


def _suffix(x, upper, pages):
    width = N_PAGES * N_HEADS
    blk = pl.BlockSpec((1, PAGE_SIZE, width), lambda b: (b, 0, 0))
    return pl.pallas_call(
        _suffix_kernel,
        grid=(DEC_BATCH,),
        in_specs=[blk, _const_spec((PAGE_SIZE, PAGE_SIZE)), _const_spec((width, width))],
        out_specs=blk,
        out_shape=jax.ShapeDtypeStruct((DEC_BATCH, PAGE_SIZE, width), F32),
        compiler_params=_params(("arbitrary",)),
        name="suffix_logf",
    )(x, upper, pages)


def _attn_sample_kernel(pt_ref, *refs):
    del pt_ref
    kp = refs[:PAGES_PER_STEP]
    vp = refs[PAGES_PER_STEP:2 * PAGES_PER_STEP]
    (d_ref, qm_ref, cb_ref, t2_ref, knew_ref, vnew_ref, gatt_ref,
     o_ref, m_scr, l_scr, acc_scr, st_scr) = refs[2 * PAGES_PER_STEP:]
    s = pl.program_id(1)
    rows = N_HEADS * DEC_SEQ
    flat = PAGE_SIZE * N_HEADS
    qm = qm_ref[0]
    cb = cb_ref[0]
    contract_last = (((1,), (1,)), ((), ()))

    @pl.when(s == 0)
    def _():
        st = lax.dot_general(qm, knew_ref[0].astype(BF16), contract_last, preferred_element_type=F32)
        r = lax.broadcasted_iota(jnp.int32, (rows, rows), 0)
        c = lax.broadcasted_iota(jnp.int32, (rows, rows), 1)
        keep = (c % N_HEADS == r // DEC_SEQ) & (c // N_HEADS <= r % DEC_SEQ)
        st = jnp.where(keep, st + cb[:, 0:rows] - t2_ref[0][:, 0:rows], NEG)
        m = jnp.max(st, axis=-1, keepdims=True)
        p = jnp.exp(st - m)
        m_scr[...] = jnp.broadcast_to(m, (rows, LANES))
        l_scr[...] = jnp.broadcast_to(jnp.sum(p, axis=-1, keepdims=True), (rows, LANES))
        acc_scr[...] = jnp.dot(p.astype(BF16), vnew_ref[0].astype(BF16), preferred_element_type=F32)

    r = lax.broadcasted_iota(jnp.int32, (rows, flat), 0)
    c = lax.broadcasted_iota(jnp.int32, (rows, flat), 1)
    same_head = (c % N_HEADS) == (r // DEC_SEQ)
    cb_wide = jnp.concatenate([cb] * (flat // LANES), axis=1)
    for j in range(PAGES_PER_STEP):
        kf = kp[j][0, 0].reshape(flat, HEAD_DIM).astype(BF16)
        st = lax.dot_general(qm, kf, contract_last, preferred_element_type=F32)
        st_scr[:, j * flat:(j + 1) * flat] = jnp.where(same_head, st + cb_wide + d_ref[0, j:j + 1, :], NEG)

    m_old = m_scr[:, 0:1]
    m_new = jnp.maximum(m_old, jnp.max(st_scr[...], axis=-1, keepdims=True))
    alpha = jnp.exp(m_old - m_new)
    p = jnp.exp(st_scr[...] - m_new)
    l_new = alpha * l_scr[:, 0:1] + jnp.sum(p, axis=-1, keepdims=True)
    pb = p.astype(BF16)
    pv = jnp.zeros((rows, HEAD_DIM), F32)
    for j in range(PAGES_PER_STEP):
        vf = vp[j][0, 0].reshape(flat, HEAD_DIM).astype(BF16)
        pv = pv + jnp.dot(pb[:, j * flat:(j + 1) * flat], vf, preferred_element_type=F32)
    acc = alpha * acc_scr[...] + pv
    m_scr[...] = jnp.broadcast_to(m_new, (rows, LANES))
    l_scr[...] = jnp.broadcast_to(l_new, (rows, LANES))
    acc_scr[...] = acc

    @pl.when(s == STEPS_PER_SEQ - 1)
    def _():
        o = acc / l_new
        att = jnp.concatenate([o[h * DEC_SEQ:(h + 1) * DEC_SEQ, :] for h in range(N_HEADS)], axis=1)
        o_ref[...] = _rms(att, gatt_ref[...])


def _attn_sample(layer, page_table, cache_k, cache_v, dflat, qm, cb, t2, knew, vnew, gatt):
    def page_spec(j):
        return pl.BlockSpec((1, 1, PAGE_SIZE, N_HEADS, HEAD_DIM),
                            lambda b, s, pt: (layer, pt[b, s * PAGES_PER_STEP + j], 0, 0, 0))

    rows = N_HEADS * DEC_SEQ
    per_seq = lambda shape: pl.BlockSpec((1,) + shape, lambda b, s, pt: (b, 0, 0))
    grid_spec = pltpu.PrefetchScalarGridSpec(
        num_scalar_prefetch=1,
        grid=(DEC_BATCH, STEPS_PER_SEQ),
        in_specs=([page_spec(j) for j in range(PAGES_PER_STEP)]
                  + [page_spec(j) for j in range(PAGES_PER_STEP)]
                  + [pl.BlockSpec((1, PAGES_PER_STEP, PAGE_SIZE * N_HEADS), lambda b, s, pt: (b, s, 0)),
                     per_seq((rows, HEAD_DIM)), per_seq((rows, LANES)), per_seq((1, LANES)),
                     per_seq((rows, HEAD_DIM)), per_seq((rows, HEAD_DIM)),
                     pl.BlockSpec((1, ATT_WIDTH), lambda b, s, pt: (0, 0))]),
        out_specs=pl.BlockSpec((DEC_SEQ, ATT_WIDTH), lambda b, s, pt: (b, 0)),
        scratch_shapes=[pltpu.VMEM((rows, LANES), F32), pltpu.VMEM((rows, LANES), F32),
                        pltpu.VMEM((rows, HEAD_DIM), F32),
                        pltpu.VMEM((rows, KEYS_PER_STEP * N_HEADS), F32)],
    )
    return pl.pallas_call(
        _attn_sample_kernel,
        grid_spec=grid_spec,
        out_shape=jax.ShapeDtypeStruct((N_SAMPLE, ATT_WIDTH), F32),
        compiler_params=_params(("arbitrary", "arbitrary")),
        name="attn_sample",
    )(page_table, *([cache_k] * PAGES_PER_STEP), *([cache_v] * PAGES_PER_STEP),
      dflat, qm, cb, t2, knew, vnew, gatt)


def _conv_kernel(u_ref, prev_ref, state_ref, w_ref, bdw_ref, gln_ref, bln_ref, gcv_ref,
                 o_ref, xs_scr, xs3_scr, y_scr):
    t = pl.program_id(0)

    @pl.when(t < NT_PROMPT)
    def _():
        first = (t % TILES_PER_SEQ) == 0
        xs_scr[0:HALO, :] = jnp.where(first, 0.0, prev_ref[...])
        xs_scr[HALO:, :] = u_ref[...]
        acc = jnp.zeros((TM, CONV_CH), F32)
        for w in range(CONV_WIDTH):
            off = w + HALO - (CONV_WIDTH - 1)
            acc = acc + xs_scr[off:off + TM, :] * w_ref[w:w + 1, :]
        y_scr[...] = acc

    @pl.when(t == NT_PROMPT)
    def _():
        xs3_scr[:, 0:HALO, :] = state_ref[...]
        xs3_scr[:, HALO:, :] = u_ref[...].reshape(DEC_BATCH, DEC_SEQ, CONV_CH)
        acc = jnp.zeros((DEC_BATCH, DEC_SEQ, CONV_CH), F32)
        for w in range(CONV_WIDTH):
            off = w + HALO - (CONV_WIDTH - 1)
            acc = acc + xs3_scr[:, off:off + DEC_SEQ, :] * w_ref[w:w + 1, :]
        y_scr[...] = acc.reshape(TM, CONV_CH)

    y = y_scr[...] + bdw_ref[...]
    yc = y - jnp.mean(y, axis=-1, keepdims=True)
    yn = yc * lax.rsqrt(jnp.mean(yc * yc, axis=-1, keepdims=True) + EPS) * gln_ref[...] + bln_ref[...]
    o_ref[...] = _rms(_silu(yn), gcv_ref[...]).astype(BF16)


def _conv(u, state_pad, w, bdw, gln, bln, gcv):
    halo_blocks = TM // HALO
    return pl.pallas_call(
        _conv_kernel,
        grid=(NT,),
        in_specs=[pl.BlockSpec((TM, CONV_CH), lambda t: (t, 0)),
                  pl.BlockSpec((HALO, CONV_CH), lambda t: (jnp.maximum(t * halo_blocks - 1, 0), 0)),
                  _const_spec((DEC_BATCH, HALO, CONV_CH)),
                  _const_spec((HALO, CONV_CH)),
                  _const_spec((1, CONV_CH)), _const_spec((1, CONV_CH)),
                  _const_spec((1, CONV_CH)), _const_spec((1, CONV_CH))],
        out_specs=pl.BlockSpec((TM, CONV_CH), lambda t: (t, 0)),
        out_shape=jax.ShapeDtypeStruct((N_TOK, CONV_CH), BF16),
        scratch_shapes=[pltpu.VMEM((TM + HALO, CONV_CH), F32),
                        pltpu.VMEM((DEC_BATCH, HALO + DEC_SEQ, CONV_CH), F32),
                        pltpu.VMEM((TM, CONV_CH), F32)],
        compiler_params=_params(("arbitrary",)),
        name="conv",
    )(u, u, state_pad, w, bdw, gln, bln, gcv)


def _top2_gates(logits):
    lane = lax.broadcasted_iota(jnp.int32, logits.shape, 1)
    m1 = jnp.max(logits, axis=-1, keepdims=True)
    i1 = jnp.min(jnp.where(logits == m1, lane, LANES), axis=-1, keepdims=True)
    rest = jnp.where(lane == i1, NEG, logits)
    m2 = jnp.max(rest, axis=-1, keepdims=True)
    i2 = jnp.min(jnp.where(rest == m2, lane, LANES), axis=-1, keepdims=True)
    e = jnp.exp(m2 - m1)
    w1 = 1.0 / (1.0 + e)
    w2 = e / (1.0 + e)
    gates = jnp.where(lane == i1, w1, 0.0) + jnp.where(lane == i2, w2, 0.0)
    chosen = jnp.where((lane == i1) | (lane == i2), 1.0, 0.0)
    return gates, chosen


def _outproj_kernel(*refs, with_router):
    if with_router:
        (h_ref, attp_ref, atts_ref, cn_ref, wa_ref, wc_ref, gffn_ref, wr_ref,
         h1_ref, n_ref, gates_ref, chosen_ref, chosen_t_ref) = refs
    else:
        h_ref, attp_ref, atts_ref, cn_ref, wa_ref, wc_ref, gffn_ref, h1_ref, n_ref = refs
    t = pl.program_id(0)
    att = jnp.where(t == NT_PROMPT, atts_ref[...].astype(BF16), attp_ref[...])
    h1 = (h_ref[...]
          + jnp.dot(att, wa_ref[...], preferred_element_type=F32)
          + jnp.dot(cn_ref[...], wc_ref[...], preferred_element_type=F32))
    h1_ref[...] = h1
    n = _rms(h1, gffn_ref[...])
    n_ref[...] = n.astype(BF16)
    if with_router:
        logits = jnp.dot(n.astype(BF16), wr_ref[...], preferred_element_type=F32)
        lane = lax.broadcasted_iota(jnp.int32, logits.shape, 1)
        gates, chosen = _top2_gates(jnp.where(lane < N_EXPERTS, logits, NEG))
        gates_ref[...] = gates
        chosen_ref[...] = chosen
        chosen_t_ref[...] = chosen.T[:N_EXPERTS]


def _outproj(h, attp, atts, cn, wa, wc, gffn, router=None):
    row = lambda t: (t, 0)
    in_specs = [pl.BlockSpec((TM, D_MODEL), row),
                pl.BlockSpec((TM, ATT_WIDTH), lambda t: (jnp.minimum(t, NT_PROMPT - 1), 0)),
                _const_spec((N_SAMPLE, ATT_WIDTH)),
                pl.BlockSpec((TM, CONV_CH), row),
                _const_spec((ATT_WIDTH, D_MODEL)), _const_spec((CONV_CH, D_MODEL)),
                _const_spec((1, D_MODEL))]
    out_specs = [pl.BlockSpec((TM, D_MODEL), row), pl.BlockSpec((TM, D_MODEL), row)]
    out_shape = [jax.ShapeDtypeStruct((N_TOK, D_MODEL), F32), jax.ShapeDtypeStruct((N_TOK, D_MODEL), BF16)]
    args = [h, attp, atts, cn, wa, wc, gffn]
    if router is not None:
        in_specs += [_const_spec((D_MODEL, LANES))]
        out_specs += [pl.BlockSpec((TM, LANES), row), pl.BlockSpec((TM, LANES), row),
                      pl.BlockSpec((N_EXPERTS, TM), lambda t: (0, t))]
        out_shape += [jax.ShapeDtypeStruct((N_TOK, LANES), F32), jax.ShapeDtypeStruct((N_TOK, LANES), F32),
                      jax.ShapeDtypeStruct((N_EXPERTS, N_TOK), F32)]
        args.append(router)
    return pl.pallas_call(
        functools.partial(_outproj_kernel, with_router=router is not None),
        grid=(NT,),
        in_specs=in_specs,
        out_specs=tuple(out_specs),
        out_shape=tuple(out_shape),
        compiler_params=_params(("arbitrary",)),
        name="outproj",
    )(*args)


def _ple(h2, p_ref, gple_ref, wproj_ref, wgate_ref):
    gate = _sigmoid(jnp.dot(_rms(h2, gple_ref[...]).astype(BF16), wgate_ref[...],
                            preferred_element_type=F32))
    proj = jnp.dot(p_ref[...].astype(BF16), wproj_ref[...], preferred_element_type=F32)
    return h2 + proj * gate


def _ffn_ple_kernel(h1_ref, n_ref, wg_ref, wu_ref, wd_ref, p_ref, gple_ref, wproj_ref, wgate_ref, o_ref):
    n = n_ref[...]
    g = jnp.dot(n, wg_ref[...], preferred_element_type=F32)
    u = jnp.dot(n, wu_ref[...], preferred_element_type=F32)
    a = (_silu(g) * u).astype(BF16)
    h2 = h1_ref[...] + jnp.dot(a, wd_ref[...], preferred_element_type=F32)
    o_ref[...] = _ple(h2, p_ref, gple_ref, wproj_ref, wgate_ref)


def _ffn_ple(h1, n, wg, wu, wd, p, gple, wproj, wgate):
    row = lambda t: (t, 0)
    return pl.pallas_call(
        _ffn_ple_kernel,
        grid=(NT,),
        in_specs=[pl.BlockSpec((TM, D_MODEL), row), pl.BlockSpec((TM, D_MODEL), row),
                  _const_spec((D_MODEL, D_FF)), _const_spec((D_MODEL, D_FF)), _const_spec((D_FF, D_MODEL)),
                  pl.BlockSpec((TM, PLE_DIM), row), _const_spec((1, D_MODEL)),
                  _const_spec((PLE_DIM, D_MODEL)), _const_spec((D_MODEL, D_MODEL))],
        out_specs=pl.BlockSpec((TM, D_MODEL), row),
        out_shape=jax.ShapeDtypeStruct((N_TOK, D_MODEL), F32),
        compiler_params=_params(("arbitrary",)),
        name="ffn_ple",
    )(h1, n, wg, wu, wd, p, gple, wproj, wgate)


def _moe_kernel(cnt_ref, h1_ref, n_ref, gates_ref, chosen_t_ref, before_ref, wg_ref, wu_ref, wd_ref,
                o_ref, acc_scr, slot_scr, xe_scr, ye_scr):
    t = pl.program_id(0)
    e = pl.program_id(1)
    c = pl.program_id(2)
    last_c = pl.num_programs(2) - 1
    n_chunks = (cnt_ref[t, e] + MOE_CH - 1) // MOE_CH

    @pl.when((e == 0) & (c == 0))
    def _():
        acc_scr[...] = h1_ref[...]
        chosen_t = chosen_t_ref[...]
        before = jnp.dot(chosen_t.astype(BF16), before_ref[...], preferred_element_type=F32)
        slot_scr[...] = jnp.where(chosen_t > 0.5, before, -1.0)

    def selection(j):
        slot = slot_scr[pl.ds(e, 1), :]
        row = (lax.broadcasted_iota(jnp.int32, (MOE_CH, MOE_TM), 0) + j * MOE_CH).astype(F32)
        return jnp.where(slot == row, 1.0, 0.0).astype(BF16)

    @pl.when(c == 0)
    def _():
        def gather(j, carry):
            r0 = pl.multiple_of(j * MOE_CH, MOE_CH)
            xe_scr[pl.ds(r0, MOE_CH), :] = jnp.dot(
                selection(j), n_ref[...], preferred_element_type=F32).astype(BF16)
            return carry
        lax.fori_loop(0, n_chunks, gather, 0)

    def expert(j, carry):
        r0 = pl.multiple_of(j * MOE_CH, MOE_CH)
        x = xe_scr[pl.ds(r0, MOE_CH), :]
        g = jnp.dot(x, wg_ref[0], preferred_element_type=F32)
        u = jnp.dot(x, wu_ref[0], preferred_element_type=F32)
        y = jnp.dot((_silu(g) * u).astype(BF16), wd_ref[0], preferred_element_type=F32)

        @pl.when(c == 0)
        def _():
            ye_scr[pl.ds(r0, MOE_CH), :] = y

        @pl.when(c > 0)
        def _():
            ye_scr[pl.ds(r0, MOE_CH), :] += y
        return carry

    lax.fori_loop(0, n_chunks, expert, 0)

    @pl.when(c == last_c)
    def _():
        lane = lax.broadcasted_iota(jnp.int32, (MOE_TM, LANES), 1)
        gate = jnp.sum(jnp.where(lane == e, gates_ref[...], 0.0), axis=-1, keepdims=True)

        def scatter(j, carry):
            r0 = pl.multiple_of(j * MOE_CH, MOE_CH)
            y = ye_scr[pl.ds(r0, MOE_CH), :]
            y_hi = y.astype(BF16)
            y_lo = (y - y_hi.astype(F32)).astype(BF16)
            sel = selection(j)
            back = lax.dot_general(jnp.concatenate([sel, sel], axis=0), jnp.concatenate([y_hi, y_lo], axis=0),
                                   (((0,), (0,)), ((), ())), preferred_element_type=F32)
            acc_scr[...] += gate * back
            return carry
        lax.fori_loop(0, n_chunks, scatter, 0)

    @pl.when((e == N_EXPERTS - 1) & (c == last_c))
    def _():
        o_ref[...] = acc_scr[...]


def _moe(counts, h1, n, gates, chosen_t, before, wg, wu, wd):
    row = lambda t, e, c, cnt: (t, 0)
    once = pl.Buffered(1)
    grid_spec = pltpu.PrefetchScalarGridSpec(
        num_scalar_prefetch=1,
        grid=(N_TOK // MOE_TM, N_EXPERTS, D_FF_EXPERT // MOE_TF),
        in_specs=[pl.BlockSpec((MOE_TM, D_MODEL), row, pipeline_mode=once),
                  pl.BlockSpec((MOE_TM, D_MODEL), row, pipeline_mode=once),
                  pl.BlockSpec((MOE_TM, LANES), row, pipeline_mode=once),
                  pl.BlockSpec((N_EXPERTS, MOE_TM), lambda t, e, c, cnt: (0, t), pipeline_mode=once),
                  pl.BlockSpec((MOE_TM, MOE_TM), lambda t, e, c, cnt: (0, 0), pipeline_mode=once),
                  pl.BlockSpec((1, D_MODEL, MOE_TF), lambda t, e, c, cnt: (e, 0, c)),
                  pl.BlockSpec((1, D_MODEL, MOE_TF), lambda t, e, c, cnt: (e, 0, c)),
                  pl.BlockSpec((1, MOE_TF, D_MODEL), lambda t, e, c, cnt: (e, c, 0))],
        out_specs=pl.BlockSpec((MOE_TM, D_MODEL), row),
        scratch_shapes=[pltpu.VMEM((MOE_TM, D_MODEL), F32),
                        pltpu.VMEM((N_EXPERTS, MOE_TM), F32),
                        pltpu.VMEM((MOE_TM, D_MODEL), BF16),
                        pltpu.VMEM((MOE_TM, D_MODEL), F32)],
    )
    return pl.pallas_call(
        _moe_kernel,
        grid_spec=grid_spec,
        out_shape=jax.ShapeDtypeStruct((N_TOK, D_MODEL), F32),
        compiler_params=_params(("arbitrary", "arbitrary", "arbitrary")),
        name="moe",
    )(counts, h1, n, gates, chosen_t, before, wg, wu, wd)


def _ple_kernel(h2_ref, p_ref, gple_ref, wproj_ref, wgate_ref, o_ref):
    o_ref[...] = _ple(h2_ref[...], p_ref, gple_ref, wproj_ref, wgate_ref)


def _ple_call(h2, p, gple, wproj, wgate):
    row = lambda t: (t, 0)
    return pl.pallas_call(
        _ple_kernel,
        grid=(NT,),
        in_specs=[pl.BlockSpec((TM, D_MODEL), row), pl.BlockSpec((TM, PLE_DIM), row),
                  _const_spec((1, D_MODEL)), _const_spec((PLE_DIM, D_MODEL)),
                  _const_spec((D_MODEL, D_MODEL))],
        out_specs=pl.BlockSpec((TM, D_MODEL), row),
        out_shape=jax.ShapeDtypeStruct((N_TOK, D_MODEL), F32),
        compiler_params=_params(("arbitrary",)),
        name="ple",
    )(h2, p, gple, wproj, wgate)


def _newkey_kernel(logf_ref, by_head_ref, by_token_ref, cb_ref, t2_ref):
    lf = logf_ref[0]
    row = lax.broadcasted_iota(jnp.int32, (DEC_SEQ, LANES), 0)
    lane = lax.broadcasted_iota(jnp.int32, (DEC_SEQ, LANES), 1)
    cn = jnp.zeros((DEC_SEQ, LANES), F32)
    for r in range(DEC_SEQ):
        cn = cn + jnp.where(row >= r, lf[r:r + 1, :], 0.0)
    pieces = _split3(cn)
    y1 = sum(jnp.dot(p, by_head_ref[...], preferred_element_type=F32) for p in pieces)
    y2 = sum(jnp.dot(p, by_token_ref[...], preferred_element_type=F32) for p in pieces)
    cb_lane = jnp.sum(jnp.where(lane % DEC_SEQ == row, y1, 0.0), axis=0, keepdims=True)
    cb_ref[0] = jnp.broadcast_to(cb_lane, (LANES, LANES)).T[0:N_HEADS * DEC_SEQ, :]
    t2_ref[0] = jnp.sum(jnp.where(lane // N_HEADS == row, y2, 0.0), axis=0, keepdims=True)


def _newkey(logf_pad, by_head, by_token):
    blk = lambda shape: pl.BlockSpec((1,) + shape, lambda b: (b, 0, 0))
    rows = N_HEADS * DEC_SEQ
    return pl.pallas_call(
        _newkey_kernel,
        grid=(DEC_BATCH,),
        in_specs=[blk((DEC_SEQ, LANES)), _const_spec((LANES, LANES)), _const_spec((LANES, LANES))],
        out_specs=(blk((rows, LANES)), blk((1, LANES))),
        out_shape=(jax.ShapeDtypeStruct((DEC_BATCH, rows, LANES), F32),
                   jax.ShapeDtypeStruct((DEC_BATCH, 1, LANES), F32)),
        compiler_params=_params(("arbitrary",)),
        name="newkey_bias",
    )(logf_pad, by_head, by_token)


def _sample_mixing(layer, page_table, cache_k, cache_v, cache_logf, q_tile, k_new, v_new, logf_new, gatt):
    rp = jnp.arange(PAGE_SIZE)
    upper = (rp[None, :] > rp[:, None]).astype(BF16)
    cw = jnp.arange(N_PAGES * N_HEADS)
    pages = ((cw[:, None] % N_HEADS == cw[None, :] % N_HEADS)
             & (cw[:, None] // N_HEADS > cw[None, :] // N_HEADS)).astype(BF16)
    ln = jnp.arange(LANES)
    used = ln[None, :] < N_HEADS * DEC_SEQ
    by_head = ((ln[:, None] == ln[None, :] // DEC_SEQ) & used).astype(BF16)
    by_token = ((ln[:, None] == ln[None, :] % N_HEADS) & used).astype(BF16)

    f_past = cache_logf[layer][page_table]
    x_suf = f_past.transpose(0, 2, 1, 3).reshape(DEC_BATCH, PAGE_SIZE, N_PAGES * N_HEADS)
    d_suf = _suffix(x_suf, upper, pages)
    dflat = (d_suf.reshape(DEC_BATCH, PAGE_SIZE, N_PAGES, N_HEADS).transpose(0, 2, 1, 3)
             .reshape(DEC_BATCH, N_PAGES, PAGE_SIZE * N_HEADS))
    q_s = q_tile.transpose(1, 0, 2).reshape(DEC_BATCH, DEC_SEQ, N_HEADS, HEAD_DIM)
    qm = q_s.transpose(0, 2, 1, 3).reshape(DEC_BATCH, N_HEADS * DEC_SEQ, HEAD_DIM)
    logf_s = jnp.pad(logf_new, ((0, 0), (0, LANES - N_HEADS))).reshape(DEC_BATCH, DEC_SEQ, LANES)
    cb, t2 = _newkey(logf_s, by_head, by_token)
    k_s = k_new.reshape(DEC_BATCH, DEC_SEQ * N_HEADS, HEAD_DIM)
    v_s = v_new.reshape(DEC_BATCH, DEC_SEQ * N_HEADS, HEAD_DIM)
    return _attn_sample(layer, page_table, cache_k, cache_v, dflat, qm, cb, t2, k_s, v_s, gatt)


def kernel(x_prompt, x_sample, cache_k, cache_v, cache_logf, state_conv, page_table, p_prompt, p_sample, g_mix, w_in, b_forget, g_q, g_k, w_dw, b_dw, g_conv_ln, b_conv_ln, g_out_att, g_out_conv, w_out, g_ffn, w_ff_gate, w_ff_up, w_ff_down, w_router, w_ex_gate, w_ex_up, w_ex_down, g_ple, w_ple_proj, w_ple_gate):
    h = jnp.concatenate([x_prompt.reshape(N_PROMPT, D_MODEL), x_sample.reshape(N_SAMPLE, D_MODEL)], axis=0)

    idx = jnp.arange(ATT_WIDTH)
    bd = jnp.where(idx[:, None] // HEAD_DIM == idx[None, :] // HEAD_DIM, 1.0 / HEAD_DIM, 0.0).astype(BF16)
    r = jnp.arange(TM)
    tri = (r[None, :] <= r[:, None]).astype(BF16)
    rm = jnp.arange(MOE_TM)
    before = (rm[:, None] < rm[None, :]).astype(BF16)

    states = []
    for i in range(DEPTH):
        w_pad = jnp.pad(w_in[i], ((0, 0), (0, IN_PAD - IN_WIDTH))).astype(BF16)
        bfp = jnp.pad(b_forget[i], (0, LANES - N_HEADS)).reshape(1, LANES)
        gq = (jnp.tile(g_q[i], N_HEADS) * ATT_SCALE).reshape(1, ATT_WIDTH)
        gk = jnp.tile(g_k[i], N_HEADS).reshape(1, ATT_WIDTH)
        q4, augq, k4, augk, vt, k, v, logf, u = _inproj(
            h, g_mix[i].reshape(1, D_MODEL), w_pad, bfp, gq, gk, bd, tri)
        gatt = g_out_att[i].reshape(1, ATT_WIDTH)

        attp = _attn_prompt(q4, augq, k4, augk.reshape(NT, TM, LANES), vt, gatt)

        atts = _sample_mixing(i, page_table, cache_k, cache_v, cache_logf, q4[NT_PROMPT],
                              k[N_PROMPT:], v[N_PROMPT:], logf[N_PROMPT:], gatt)

        state_pad = jnp.pad(state_conv[i], ((0, 0), (HALO - (CONV_WIDTH - 1), 0), (0, 0)))
        w_dw_pad = jnp.pad(w_dw[i], ((0, HALO - CONV_WIDTH), (0, 0)))
        cn = _conv(u, state_pad, w_dw_pad, b_dw[i].reshape(1, CONV_CH), g_conv_ln[i].reshape(1, CONV_CH),
                   b_conv_ln[i].reshape(1, CONV_CH), g_out_conv[i].reshape(1, CONV_CH))

        wo = w_out[i].astype(BF16)
        p_all = jnp.concatenate([p_prompt[i].reshape(N_PROMPT, PLE_DIM), p_sample[i].reshape(N_SAMPLE, PLE_DIM)], axis=0)
        gple = g_ple[i].reshape(1, D_MODEL)
        wproj = w_ple_proj[i].astype(BF16)
        wgate = w_ple_gate[i].astype(BF16)
        gffn = g_ffn[i].reshape(1, D_MODEL)
        j = i // 2
        if i % 2 == 0:
            h1, n = _outproj(h, attp, atts, cn, wo[:ATT_WIDTH], wo[ATT_WIDTH:], gffn)
            h = _ffn_ple(h1, n, w_ff_gate[j].astype(BF16), w_ff_up[j].astype(BF16),
                         w_ff_down[j].astype(BF16), p_all, gple, wproj, wgate)
        else:
            wr = jnp.pad(w_router[j], ((0, 0), (0, LANES - N_EXPERTS))).astype(BF16)
            h1, n, gates, chosen, chosen_t = _outproj(h, attp, atts, cn, wo[:ATT_WIDTH], wo[ATT_WIDTH:], gffn,
                                                      router=wr)
            counts = jnp.sum(chosen[:, :N_EXPERTS].reshape(N_TOK // MOE_TM, MOE_TM, N_EXPERTS),
                             axis=1).astype(jnp.int32)
            h2 = _moe(counts, h1, n, gates, chosen_t, before, w_ex_gate[j].astype(BF16),
                      w_ex_up[j].astype(BF16), w_ex_down[j].astype(BF16))
            h = _ple_call(h2, p_all, gple, wproj, wgate)

        conv_p = u[:N_PROMPT].reshape(BATCH, SEQ, CONV_CH)[:, SEQ - (CONV_WIDTH - 1):]
        conv_s = jnp.concatenate([state_conv[i], u[N_PROMPT:].reshape(DEC_BATCH, DEC_SEQ, CONV_CH)],
                                 axis=1)[:, DEC_SEQ:]
        states.append((k, v, logf, conv_p, conv_s))

    def stack(idx, lo, hi, shape):
        return jnp.stack([s[idx][lo:hi].reshape(shape) for s in states])

    kv_p = (BATCH, SEQ, N_HEADS, HEAD_DIM)
    kv_s = (DEC_BATCH, DEC_SEQ, N_HEADS, HEAD_DIM)
    return (h[:N_PROMPT].reshape(BATCH, SEQ, D_MODEL),
            h[N_PROMPT:].reshape(DEC_BATCH, DEC_SEQ, D_MODEL),
            stack(0, 0, N_PROMPT, kv_p), stack(1, 0, N_PROMPT, kv_p),
            stack(2, 0, N_PROMPT, (BATCH, SEQ, N_HEADS)),
            jnp.stack([s[3] for s in states]),
            stack(0, N_PROMPT, N_TOK, kv_s), stack(1, N_PROMPT, N_TOK, kv_s),
            stack(2, N_PROMPT, N_TOK, (DEC_BATCH, DEC_SEQ, N_HEADS)),
            jnp.stack([s[4] for s in states]))
```

```python
import functools

import jax
import jax.numpy as jnp
from jax import lax
from jax.experimental import pallas as pl
from jax.experimental.pallas import tpu as pltpu

F32 = jnp.float32
BF16 = jnp.bfloat16

D_MODEL = 1024
BATCH = 2
SEQ = 8192
DEPTH = 4
DEC_BATCH = 32
DEC_SEQ = 8
PAST_LEN = 8192
PAGE_SIZE = 128
N_PAGES = PAST_LEN // PAGE_SIZE
N_HEADS = 8
HEAD_DIM = 64
ATT_WIDTH = N_HEADS * HEAD_DIM
CONV_CH = D_MODEL // 2
CONV_WIDTH = 31
IN_WIDTH = 3 * ATT_WIDTH + 2 * CONV_CH + N_HEADS
D_FF = 2816
N_EXPERTS = 8
D_FF_EXPERT = D_MODEL * 7 // 2
PLE_DIM = 256
EPS = 1e-6
ATT_SCALE = HEAD_DIM ** -0.5

LANES = 128
N_PROMPT = BATCH * SEQ
N_SAMPLE = DEC_BATCH * DEC_SEQ
N_TOK = N_PROMPT + N_SAMPLE
TM = 256
NT = N_TOK // TM
NT_PROMPT = N_PROMPT // TM
TILES_PER_SEQ = SEQ // TM
IN_PAD = 3 * ATT_WIDTH + 2 * CONV_CH + LANES
VT_ROWS = HEAD_DIM + 16
HALO = 32
PAGES_PER_STEP = 8
KEYS_PER_STEP = PAGES_PER_STEP * PAGE_SIZE
STEPS_PER_SEQ = N_PAGES // PAGES_PER_STEP
MOE_TM = 1280
MOE_TF = 896
MOE_CH = 128
NEG = -1e30
VMEM_LIMIT = 56 * 1024 * 1024


def _params(sem, vmem=VMEM_LIMIT):
    return pltpu.CompilerParams(dimension_semantics=sem, vmem_limit_bytes=vmem)


def _const_spec(shape):
    nd = len(shape)
    return pl.BlockSpec(shape, lambda *_: (0,) * nd, pipeline_mode=pl.Buffered(1))


def _split3(x):
    hi = x.astype(BF16)
    r = x - hi.astype(F32)
    mid = r.astype(BF16)
    lo = (r - mid.astype(F32)).astype(BF16)
    return hi, mid, lo


def _rms(x, g):
    return x * lax.rsqrt(jnp.mean(x * x, axis=-1, keepdims=True) + EPS) * g


def _sigmoid(x):
    return 1.0 / (1.0 + jnp.exp(-x))


def _silu(x):
    return x * _sigmoid(x)


def _log_sigmoid(x):
    return jnp.minimum(x, 0.0) - jnp.log1p(jnp.exp(-jnp.abs(x)))


def _inproj_kernel(h_ref, gmix_ref, w_ref, bf_ref, gq_ref, gk_ref, bd_ref, tri_ref,
                   q4_ref, augq_ref, k4_ref, augk_ref, vt_ref, k_ref, v_ref, logf_ref, u_ref,
                   carry_ref):
    i = pl.program_id(0)
    xn = _rms(h_ref[...], gmix_ref[...]).astype(BF16)

    def proj(c0, c1):
        return jnp.dot(xn, w_ref[:, c0:c1], preferred_element_type=F32)

    def head_norm(z, g):
        ms = jnp.dot((z * z).astype(BF16), bd_ref[...], preferred_element_type=F32)
        return z * lax.rsqrt(ms + EPS) * g

    q = head_norm(proj(0, 512), gq_ref[...])
    k = head_norm(proj(512, 1024), gk_ref[...])
    v = proj(1024, 1536)
    ua = proj(1536, 2048)
    ug = proj(2048, 2560)
    zf = proj(2560, IN_PAD)

    lane = lax.broadcasted_iota(jnp.int32, (TM, LANES), 1)
    logf = jnp.where(lane < N_HEADS, _log_sigmoid(zf + bf_ref[...]), 0.0)

    k_ref[...] = k
    v_ref[...] = v
    u_ref[...] = ua * _sigmoid(ug)
    logf_ref[...] = logf[:, :N_HEADS]

    qb = q.astype(BF16)
    kb = k.astype(BF16)
    for j in range(4):
        q4_ref[0, j] = qb[:, j * LANES:(j + 1) * LANES]
        k4_ref[0, j] = kb[:, j * LANES:(j + 1) * LANES]
    vt_ref[0, :, 0:HEAD_DIM, :] = v.T.astype(BF16).reshape(N_HEADS, HEAD_DIM, TM)
    vt_ref[0, :, HEAD_DIM:VT_ROWS, :] = jnp.ones((N_HEADS, VT_ROWS - HEAD_DIM, TM), BF16)

    hi, mid, lo = _split3(logf)
    tri = tri_ref[...]
    c = (jnp.dot(tri, hi, preferred_element_type=F32)
         + jnp.dot(tri, mid, preferred_element_type=F32)
         + jnp.dot(tri, lo, preferred_element_type=F32))
    carry = jnp.where(i % TILES_PER_SEQ == 0, 0.0, carry_ref[0:1, :])
    c = c + carry
    carry_ref[...] = jnp.broadcast_to(c[TM - 1:TM, :], (8, LANES))

    chi, cmid, clo = (p.astype(F32) for p in _split3(c))
    ones_q = jnp.where((lane >= 24) & (lane < 48), 1.0, 0.0)
    ones_k = jnp.where(lane < 24, 1.0, 0.0)
    augq = chi + pltpu.roll(cmid, 8, 1) + pltpu.roll(clo, 16, 1) + ones_q
    augk = ones_k - pltpu.roll(chi, 24, 1) - pltpu.roll(cmid, 32, 1) - pltpu.roll(clo, 40, 1)
    augq_ref[...] = augq.astype(BF16)
    augk_ref[...] = augk.astype(BF16)


def _inproj(h, gmix, w, bfp, gq, gk, bd, tri):
    row = lambda i: (i, 0)
    out_shape = (
        jax.ShapeDtypeStruct((NT, 4, TM, LANES), BF16),
        jax.ShapeDtypeStruct((N_TOK, LANES), BF16),
        jax.ShapeDtypeStruct((NT, 4, TM, LANES), BF16),
        jax.ShapeDtypeStruct((N_TOK, LANES), BF16),
        jax.ShapeDtypeStruct((NT, N_HEADS, VT_ROWS, TM), BF16),
        jax.ShapeDtypeStruct((N_TOK, ATT_WIDTH), F32),
        jax.ShapeDtypeStruct((N_TOK, ATT_WIDTH), F32),
        jax.ShapeDtypeStruct((N_TOK, N_HEADS), F32),
        jax.ShapeDtypeStruct((N_TOK, CONV_CH), F32),
    )
    out_specs = (
        pl.BlockSpec((1, 4, TM, LANES), lambda i: (i, 0, 0, 0)),
        pl.BlockSpec((TM, LANES), row),
        pl.BlockSpec((1, 4, TM, LANES), lambda i: (i, 0, 0, 0)),
        pl.BlockSpec((TM, LANES), row),
        pl.BlockSpec((1, N_HEADS, VT_ROWS, TM), lambda i: (i, 0, 0, 0)),
        pl.BlockSpec((TM, ATT_WIDTH), row),
        pl.BlockSpec((TM, ATT_WIDTH), row),
        pl.BlockSpec((TM, N_HEADS), row),
        pl.BlockSpec((TM, CONV_CH), row),
    )
    return pl.pallas_call(
        _inproj_kernel,
        grid=(NT,),
        in_specs=[pl.BlockSpec((TM, D_MODEL), row),
                  _const_spec((1, D_MODEL)), _const_spec((D_MODEL, IN_PAD)), _const_spec((1, LANES)),
                  _const_spec((1, ATT_WIDTH)), _const_spec((1, ATT_WIDTH)),
                  _const_spec((ATT_WIDTH, ATT_WIDTH)), _const_spec((TM, TM))],
        out_specs=out_specs,
        out_shape=out_shape,
        scratch_shapes=[pltpu.VMEM((8, LANES), F32)],
        compiler_params=_params(("arbitrary",)),
        name="inproj",
    )(h, gmix, w, bfp, gq, gk, bd, tri)


def _attn_kernel(q4_ref, augq_ref, k4_ref, augk_ref, vt_ref, gatt_ref, o_ref, qa_scr, m_scr, acc_scr, st_scr):
    qi = pl.program_id(1)
    lane = lax.broadcasted_iota(jnp.int32, (TM, LANES), 1)
    augq = augq_ref[...]
    zero = jnp.zeros((), BF16)
    for h in range(N_HEADS):
        qm = jnp.where((lane // HEAD_DIM) == (h % 2), q4_ref[0, h // 2], zero)
        am = jnp.where(((lane % 8) == h) & (lane < 48), augq, zero)
        qa_scr[h] = jnp.concatenate([qm, am], axis=1)
    m_scr[...] = jnp.full((N_HEADS, 1, TM), NEG, F32)
    acc_scr[...] = jnp.zeros((N_HEADS, VT_ROWS, TM), F32)

    key_pos = lax.broadcasted_iota(jnp.int32, (TM, TM), 0)
    qry_pos = lax.broadcasted_iota(jnp.int32, (TM, TM), 1)
    causal = key_pos <= qry_pos

    def scores(kb, slot):
        aug = augk_ref[kb]
        keep = causal | (kb < qi)
        for h in range(N_HEADS):
            ka = jnp.concatenate([k4_ref[kb, h // 2], aug], axis=1)
            st = lax.dot_general(ka, qa_scr[h], (((1,), (1,)), ((), ())),
                                 preferred_element_type=F32)
            st_scr[slot, h] = jnp.where(keep, st, NEG)

    def softmax_pv(kb, slot):
        for h in range(N_HEADS):
            m = m_scr[h]
            m_new = jnp.maximum(m, jnp.max(st_scr[slot, h], axis=0, keepdims=True))
            p = jnp.exp(st_scr[slot, h] - m_new).astype(BF16)
            pv = jnp.dot(vt_ref[kb, h], p, preferred_element_type=F32)
            acc_scr[h] = jnp.exp(m - m_new) * acc_scr[h] + pv
            m_scr[h] = m_new

    scores(0, 0)

    def body(kk, carry):
        kb = 2 * kk
        scores(kb + 1, 1)
        softmax_pv(kb, 0)
        scores(kb + 2, 0)
        softmax_pv(kb + 1, 1)
        return carry

    lax.fori_loop(0, qi // 2, body, 0)

    @pl.when(qi % 2 == 0)
    def _():
        softmax_pv(qi, 0)

    @pl.when(qi % 2 == 1)
    def _():
        scores(qi, 1)
        softmax_pv(qi - 1, 0)
        softmax_pv(qi, 1)

    acc = acc_scr[...]
    ot = acc[:, 0:HEAD_DIM, :] / acc[:, HEAD_DIM:HEAD_DIM + 1, :]
    o = ot.reshape(ATT_WIDTH, TM).T
    o_ref[...] = _rms(o, gatt_ref[...]).astype(BF16)


def _attn_prompt(q4, augq, k4, augk3, vt, gatt):
    seq_blk = lambda b, qi: (b, 0, 0, 0)
    once = pl.Buffered(1)
    return pl.pallas_call(
        _attn_kernel,
        grid=(BATCH, TILES_PER_SEQ),
        in_specs=[pl.BlockSpec((1, 4, TM, LANES), lambda b, qi: (b * TILES_PER_SEQ + qi, 0, 0, 0)),
                  pl.BlockSpec((TM, LANES), lambda b, qi: (b * TILES_PER_SEQ + qi, 0)),
                  pl.BlockSpec((TILES_PER_SEQ, 4, TM, LANES), seq_blk, pipeline_mode=once),
                  pl.BlockSpec((TILES_PER_SEQ, TM, LANES), lambda b, qi: (b, 0, 0), pipeline_mode=once),
                  pl.BlockSpec((TILES_PER_SEQ, N_HEADS, VT_ROWS, TM), seq_blk, pipeline_mode=once),
                  _const_spec((1, ATT_WIDTH))],
        out_specs=pl.BlockSpec((TM, ATT_WIDTH), lambda b, qi: (b * TILES_PER_SEQ + qi, 0)),
        out_shape=jax.ShapeDtypeStruct((N_PROMPT, ATT_WIDTH), BF16),
        scratch_shapes=[pltpu.VMEM((N_HEADS, TM, 2 * LANES), BF16),
                        pltpu.VMEM((N_HEADS, 1, TM), F32),
                        pltpu.VMEM((N_HEADS, VT_ROWS, TM), F32),
                        pltpu.VMEM((2, N_HEADS, TM, TM), F32)],
        compiler_params=_params(("arbitrary", "arbitrary")),
        name="attn_prompt",
    )(q4, augq, k4, augk3, vt, gatt)


def _attn_sample_kernel(pt_ref, *refs):
    del pt_ref
    n = PAGES_PER_STEP
    kp, vp, fp = refs[:n], refs[n:2 * n], refs[2 * n:3 * n]
    (qblk_ref, lfn_ref, knew_ref, vnew_ref, upper_ref, tri_ref, gatt_ref,
     o_ref, m_scr, l_scr, acc_scr, cb_scr, carry_scr) = refs[3 * n:]
    s = pl.program_id(1)
    rows = DEC_SEQ * N_HEADS
    qblk = qblk_ref[0]
    row = lax.broadcasted_iota(jnp.int32, (rows, LANES), 0)
    lane = lax.broadcasted_iota(jnp.int32, (rows, LANES), 1)
    contract_last = (((1,), (1,)), ((), ()))

    def over_tokens(x, mat_ref):
        return sum(jnp.dot(p, mat_ref[...], preferred_element_type=F32) for p in _split3(x))

    @pl.when(s == 0)
    def _():
        cn_t = over_tokens(lfn_ref[0], tri_ref)
        cn_rows = jnp.concatenate([cn_t] * DEC_SEQ, axis=0)
        cb = jnp.sum(jnp.where(lane == row // N_HEADS, cn_rows, 0.0), axis=-1, keepdims=True)
        cb_scr[...] = jnp.broadcast_to(cb, (rows, LANES))
        carry_scr[...] = jnp.zeros((N_HEADS, LANES), F32)
        pad = jnp.zeros((LANES - DEC_SEQ, ATT_WIDTH), BF16)
        kn = jnp.concatenate([knew_ref[0].astype(BF16), pad], axis=0)
        vn = jnp.concatenate([vnew_ref[0].astype(BF16), pad], axis=0)
        st = lax.dot_general(qblk, kn, contract_last, preferred_element_type=F32)
        st = jnp.where((lane < DEC_SEQ) & (lane <= row // N_HEADS), st + cb - cn_rows, NEG)
        m = jnp.max(st, axis=-1, keepdims=True)
        p = jnp.exp(st - m)
        m_scr[...] = jnp.broadcast_to(m, (rows, LANES))
        l_scr[...] = jnp.broadcast_to(jnp.sum(p, axis=-1, keepdims=True), (rows, LANES))
        acc_scr[...] = jnp.dot(p.astype(BF16), vn, preferred_element_type=F32)

    lf = jnp.concatenate([r[0, 0] for r in fp], axis=0)
    inside = over_tokens(lf, upper_ref)
    total = jnp.sum(lf, axis=-1, keepdims=True)
    carry = carry_scr[...]
    bias = []
    for j in range(n):
        d = inside[j * N_HEADS:(j + 1) * N_HEADS] + carry
        carry = carry + total[j * N_HEADS:(j + 1) * N_HEADS]
        bias.append(jnp.concatenate([d] * DEC_SEQ, axis=0))
    carry_scr[...] = carry
    cb = cb_scr[...]
    bias = jnp.concatenate([b + cb for b in bias], axis=1)

    kt = jnp.concatenate([r[0, 0].reshape(ATT_WIDTH, PAGE_SIZE).astype(BF16) for r in kp], axis=1)
    st = jnp.dot(qblk, kt, preferred_element_type=F32) + bias
    m_old = m_scr[:, 0:1]
    m_new = jnp.maximum(m_old, jnp.max(st, axis=-1, keepdims=True))
    alpha = jnp.exp(m_old - m_new)
    p = jnp.exp(st - m_new)
    l_new = alpha * l_scr[:, 0:1] + jnp.sum(p, axis=-1, keepdims=True)
    vt = jnp.concatenate([r[0, 0].reshape(ATT_WIDTH, PAGE_SIZE).astype(BF16) for r in vp], axis=1)
    pv = lax.dot_general(p.astype(BF16), vt, contract_last, preferred_element_type=F32)
    acc = alpha * acc_scr[...] + pv
    m_scr[...] = jnp.broadcast_to(m_new, (rows, LANES))
    l_scr[...] = jnp.broadcast_to(l_new, (rows, LANES))
    acc_scr[...] = acc

    @pl.when(s == STEPS_PER_SEQ - 1)
    def _():
        o = acc / l_new
        r2 = lax.broadcasted_iota(jnp.int32, (rows, ATT_WIDTH), 0)
        c2 = lax.broadcasted_iota(jnp.int32, (rows, ATT_WIDTH), 1)
        o = jnp.where(c2 // HEAD_DIM == r2 % N_HEADS, o, 0.0)
        att = jnp.sum(o.reshape(DEC_SEQ, N_HEADS, ATT_WIDTH), axis=1)
        o_ref[...] = _rms(att, gatt_ref[...])


def _attn_sample(layer, page_table, k_t, v_t, lf_t, qblk, lfn, knew, vnew, upper, tri, gatt):
    def page(j):
        return lambda b, s, pt: pt[b, N_PAGES - 1 - (s * PAGES_PER_STEP + j)]

    def kv_spec(j):
        return pl.BlockSpec((1, 1, N_HEADS, HEAD_DIM, PAGE_SIZE),
                            lambda b, s, pt: (layer, page(j)(b, s, pt), 0, 0, 0))

    def lf_spec(j):
        return pl.BlockSpec((1, 1, N_HEADS, PAGE_SIZE), lambda b, s, pt: (layer, page(j)(b, s, pt), 0, 0))

    rows = DEC_SEQ * N_HEADS
    per_seq = lambda shape: pl.BlockSpec((1,) + shape, lambda b, s, pt: (b, 0, 0))
    const = lambda shape: pl.BlockSpec(shape, lambda b, s, pt: (0, 0))
    pages = range(PAGES_PER_STEP)
    grid_spec = pltpu.PrefetchScalarGridSpec(
        num_scalar_prefetch=1,
        grid=(DEC_BATCH, STEPS_PER_SEQ),
        in_specs=([kv_spec(j) for j in pages] + [kv_spec(j) for j in pages] + [lf_spec(j) for j in pages]
                  + [per_seq((rows, ATT_WIDTH)), per_seq((N_HEADS, LANES)),
                     per_seq((DEC_SEQ, ATT_WIDTH)), per_seq((DEC_SEQ, ATT_WIDTH)),
                     const((PAGE_SIZE, PAGE_SIZE)), const((LANES, LANES)), const((1, ATT_WIDTH))]),
        out_specs=pl.BlockSpec((DEC_SEQ, ATT_WIDTH), lambda b, s, pt: (b, 0)),
        scratch_shapes=[pltpu.VMEM((rows, LANES), F32), pltpu.VMEM((rows, LANES), F32),
                        pltpu.VMEM((rows, ATT_WIDTH), F32), pltpu.VMEM((rows, LANES), F32),
                        pltpu.VMEM((N_HEADS, LANES), F32)],
    )
    return pl.pallas_call(
        _attn_sample_kernel,
        grid_spec=grid_spec,
        out_shape=jax.ShapeDtypeStruct((N_SAMPLE, ATT_WIDTH), F32),
        compiler_params=_params(("arbitrary", "arbitrary")),
        name="attn_sample",
    )(page_table, *([k_t] * PAGES_PER_STEP), *([v_t] * PAGES_PER_STEP), *([lf_t] * PAGES_PER_STEP),
      qblk, lfn, knew, vnew, upper, tri, gatt)


def _conv_kernel(u_ref, prev_ref, state_ref, w_ref, bdw_ref, gln_ref, bln_ref, gcv_ref,
                 o_ref, xs_scr, xs3_scr, y_scr):
    t = pl.program_id(0)

    @pl.when(t < NT_PROMPT)
    def _():
        first = (t % TILES_PER_SEQ) == 0
        xs_scr[0:HALO, :] = jnp.where(first, 0.0, prev_ref[...])
        xs_scr[HALO:, :] = u_ref[...]
        acc = jnp.zeros((TM, CONV_CH), F32)
        for w in range(CONV_WIDTH):
            off = w + HALO - (CONV_WIDTH - 1)
            acc = acc + xs_scr[off:off + TM, :] * w_ref[w:w + 1, :]
        y_scr[...] = acc

    @pl.when(t == NT_PROMPT)
    def _():
        xs3_scr[:, 0:HALO, :] = state_ref[...]
        xs3_scr[:, HALO:, :] = u_ref[...].reshape(DEC_BATCH, DEC_SEQ, CONV_CH)
        acc = jnp.zeros((DEC_BATCH, DEC_SEQ, CONV_CH), F32)
        for w in range(CONV_WIDTH):
            off = w + HALO - (CONV_WIDTH - 1)
            acc = acc + xs3_scr[:, off:off + DEC_SEQ, :] * w_ref[w:w + 1, :]
        y_scr[...] = acc.reshape(TM, CONV_CH)

    y = y_scr[...] + bdw_ref[...]
    yc = y - jnp.mean(y, axis=-1, keepdims=True)
    yn = yc * lax.rsqrt(jnp.mean(yc * yc, axis=-1, keepdims=True) + EPS) * gln_ref[...] + bln_ref[...]
    o_ref[...] = _rms(_silu(yn), gcv_ref[...]).astype(BF16)


def _conv(u, state_pad, w, bdw, gln, bln, gcv):
    halo_blocks = TM // HALO
    return pl.pallas_call(
        _conv_kernel,
        grid=(NT,),
        in_specs=[pl.BlockSpec((TM, CONV_CH), lambda t: (t, 0)),
                  pl.BlockSpec((HALO, CONV_CH), lambda t: (jnp.maximum(t * halo_blocks - 1, 0), 0)),
                  _const_spec((DEC_BATCH, HALO, CONV_CH)),
                  _const_spec((HALO, CONV_CH)),
                  _const_spec((1, CONV_CH)), _const_spec((1, CONV_CH)),
                  _const_spec((1, CONV_CH)), _const_spec((1, CONV_CH))],
        out_specs=pl.BlockSpec((TM, CONV_CH), lambda t: (t, 0)),
        out_shape=jax.ShapeDtypeStruct((N_TOK, CONV_CH), BF16),
        scratch_shapes=[pltpu.VMEM((TM + HALO, CONV_CH), F32),
                        pltpu.VMEM((DEC_BATCH, HALO + DEC_SEQ, CONV_CH), F32),
                        pltpu.VMEM((TM, CONV_CH), F32)],
        compiler_params=_params(("arbitrary",)),
        name="conv",
    )(u, u, state_pad, w, bdw, gln, bln, gcv)


def _top2_gates(logits):
    lane = lax.broadcasted_iota(jnp.int32, logits.shape, 1)
    m1 = jnp.max(logits, axis=-1, keepdims=True)
    i1 = jnp.min(jnp.where(logits == m1, lane, LANES), axis=-1, keepdims=True)
    rest = jnp.where(lane == i1, NEG, logits)
    m2 = jnp.max(rest, axis=-1, keepdims=True)
    i2 = jnp.min(jnp.where(rest == m2, lane, LANES), axis=-1, keepdims=True)
    e = jnp.exp(m2 - m1)
    w1 = 1.0 / (1.0 + e)
    w2 = e / (1.0 + e)
    gates = jnp.where(lane == i1, w1, 0.0) + jnp.where(lane == i2, w2, 0.0)
    chosen = jnp.where((lane == i1) | (lane == i2), 1.0, 0.0)
    return gates, chosen


def _outproj_kernel(*refs, with_router):
    if with_router:
        (h_ref, attp_ref, atts_ref, cn_ref, wa_ref, wc_ref, gffn_ref, wr_ref,
         h1_ref, n_ref, gates_ref, chosen_ref, chosen_t_ref) = refs
    else:
        h_ref, attp_ref, atts_ref, cn_ref, wa_ref, wc_ref, gffn_ref, h1_ref, n_ref = refs
    t = pl.program_id(0)
    att = jnp.where(t == NT_PROMPT, atts_ref[...].astype(BF16), attp_ref[...])
    h1 = (h_ref[...]
          + jnp.dot(att, wa_ref[...], preferred_element_type=F32)
          + jnp.dot(cn_ref[...], wc_ref[...], preferred_element_type=F32))
    h1_ref[...] = h1
    n = _rms(h1, gffn_ref[...])
    n_ref[...] = n.astype(BF16)
    if with_router:
        logits = jnp.dot(n.astype(BF16), wr_ref[...], preferred_element_type=F32)
        lane = lax.broadcasted_iota(jnp.int32, logits.shape, 1)
        gates, chosen = _top2_gates(jnp.where(lane < N_EXPERTS, logits, NEG))
        gates_ref[...] = gates
        chosen_ref[...] = chosen
        chosen_t_ref[...] = chosen.T[:N_EXPERTS]


def _outproj(h, attp, atts, cn, wa, wc, gffn, router=None):
    row = lambda t: (t, 0)
    in_specs = [pl.BlockSpec((TM, D_MODEL), row),
                pl.BlockSpec((TM, ATT_WIDTH), lambda t: (jnp.minimum(t, NT_PROMPT - 1), 0)),
                _const_spec((N_SAMPLE, ATT_WIDTH)),
                pl.BlockSpec((TM, CONV_CH), row),
                _const_spec((ATT_WIDTH, D_MODEL)), _const_spec((CONV_CH, D_MODEL)),
                _const_spec((1, D_MODEL))]
    out_specs = [pl.BlockSpec((TM, D_MODEL), row), pl.BlockSpec((TM, D_MODEL), row)]
    out_shape = [jax.ShapeDtypeStruct((N_TOK, D_MODEL), F32), jax.ShapeDtypeStruct((N_TOK, D_MODEL), BF16)]
    args = [h, attp, atts, cn, wa, wc, gffn]
    if router is not None:
        in_specs += [_const_spec((D_MODEL, LANES))]
        out_specs += [pl.BlockSpec((TM, LANES), row), pl.BlockSpec((TM, LANES), row),
                      pl.BlockSpec((N_EXPERTS, TM), lambda t: (0, t))]
        out_shape += [jax.ShapeDtypeStruct((N_TOK, LANES), F32), jax.ShapeDtypeStruct((N_TOK, LANES), F32),
                      jax.ShapeDtypeStruct((N_EXPERTS, N_TOK), F32)]
        args.append(router)
    return pl.pallas_call(
        functools.partial(_outproj_kernel, with_router=router is not None),
        grid=(NT,),
        in_specs=in_specs,
        out_specs=tuple(out_specs),
        out_shape=tuple(out_shape),
        compiler_params=_params(("arbitrary",)),
        name="outproj",
    )(*args)


def _ple(h2, p_ref, gple_ref, wproj_ref, wgate_ref):
    gate = _sigmoid(jnp.dot(_rms(h2, gple_ref[...]).astype(BF16), wgate_ref[...],
                            preferred_element_type=F32))
    proj = jnp.dot(p_ref[...].astype(BF16), wproj_ref[...], preferred_element_type=F32)
    return h2 + proj * gate


def _ffn_ple_kernel(h1_ref, n_ref, wg_ref, wu_ref, wd_ref, p_ref, gple_ref, wproj_ref, wgate_ref, o_ref):
    n = n_ref[...]
    g = jnp.dot(n, wg_ref[...], preferred_element_type=F32)
    u = jnp.dot(n, wu_ref[...], preferred_element_type=F32)
    a = (_silu(g) * u).astype(BF16)
    h2 = h1_ref[...] + jnp.dot(a, wd_ref[...], preferred_element_type=F32)
    o_ref[...] = _ple(h2, p_ref, gple_ref, wproj_ref, wgate_ref)


def _ffn_ple(h1, n, wg, wu, wd, p, gple, wproj, wgate):
    row = lambda t: (t, 0)
    return pl.pallas_call(
        _ffn_ple_kernel,
        grid=(NT,),
        in_specs=[pl.BlockSpec((TM, D_MODEL), row), pl.BlockSpec((TM, D_MODEL), row),
                  _const_spec((D_MODEL, D_FF)), _const_spec((D_MODEL, D_FF)), _const_spec((D_FF, D_MODEL)),
                  pl.BlockSpec((TM, PLE_DIM), row), _const_spec((1, D_MODEL)),
                  _const_spec((PLE_DIM, D_MODEL)), _const_spec((D_MODEL, D_MODEL))],
        out_specs=pl.BlockSpec((TM, D_MODEL), row),
        out_shape=jax.ShapeDtypeStruct((N_TOK, D_MODEL), F32),
        compiler_params=_params(("arbitrary",)),
        name="ffn_ple",
    )(h1, n, wg, wu, wd, p, gple, wproj, wgate)


def _moe_kernel(cnt_ref, h1_ref, n_ref, gates_ref, chosen_t_ref, before_ref, wg_ref, wu_ref, wd_ref,
                o_ref, acc_scr, slot_scr, xe_scr, ye_scr):
    t = pl.program_id(0)
    e = pl.program_id(1)
    c = pl.program_id(2)
    last_c = pl.num_programs(2) - 1
    n_chunks = (cnt_ref[t, e] + MOE_CH - 1) // MOE_CH

    @pl.when((e == 0) & (c == 0))
    def _():
        acc_scr[...] = h1_ref[...]
        chosen_t = chosen_t_ref[...]
        before = jnp.dot(chosen_t.astype(BF16), before_ref[...], preferred_element_type=F32)
        slot_scr[...] = jnp.where(chosen_t > 0.5, before, -1.0)

    def selection(j):
        slot = slot_scr[pl.ds(e, 1), :]
        row = (lax.broadcasted_iota(jnp.int32, (MOE_CH, MOE_TM), 0) + j * MOE_CH).astype(F32)
        return jnp.where(slot == row, 1.0, 0.0).astype(BF16)

    @pl.when(c == 0)
    def _():
        def gather(j, carry):
            r0 = pl.multiple_of(j * MOE_CH, MOE_CH)
            xe_scr[pl.ds(r0, MOE_CH), :] = jnp.dot(
                selection(j), n_ref[...], preferred_element_type=F32).astype(BF16)
            return carry
        lax.fori_loop(0, n_chunks, gather, 0)

    def expert(j, carry):
        r0 = pl.multiple_of(j * MOE_CH, MOE_CH)
        x = xe_scr[pl.ds(r0, MOE_CH), :]
        g = jnp.dot(x, wg_ref[0], preferred_element_type=F32)
        u = jnp.dot(x, wu_ref[0], preferred_element_type=F32)
        y = jnp.dot((_silu(g) * u).astype(BF16), wd_ref[0], preferred_element_type=F32)

        @pl.when(c == 0)
        def _():
            ye_scr[pl.ds(r0, MOE_CH), :] = y

        @pl.when(c > 0)
        def _():
            ye_scr[pl.ds(r0, MOE_CH), :] += y
        return carry

    lax.fori_loop(0, n_chunks, expert, 0)

    @pl.when(c == last_c)
    def _():
        lane = lax.broadcasted_iota(jnp.int32, (MOE_TM, LANES), 1)
        gate = jnp.sum(jnp.where(lane == e, gates_ref[...], 0.0), axis=-1, keepdims=True)

        def scatter(j, carry):
            r0 = pl.multiple_of(j * MOE_CH, MOE_CH)
            y = ye_scr[pl.ds(r0, MOE_CH), :]
            y_hi = y.astype(BF16)
            y_lo = (y - y_hi.astype(F32)).astype(BF16)
            sel = selection(j)
            back = lax.dot_general(jnp.concatenate([sel, sel], axis=0), jnp.concatenate([y_hi, y_lo], axis=0),
                                   (((0,), (0,)), ((), ())), preferred_element_type=F32)
            acc_scr[...] += gate * back
            return carry
        lax.fori_loop(0, n_chunks, scatter, 0)

    @pl.when((e == N_EXPERTS - 1) & (c == last_c))
    def _():
        o_ref[...] = acc_scr[...]


def _moe(counts, h1, n, gates, chosen_t, before, wg, wu, wd):
    row = lambda t, e, c, cnt: (t, 0)
    once = pl.Buffered(1)
    grid_spec = pltpu.PrefetchScalarGridSpec(
        num_scalar_prefetch=1,
        grid=(N_TOK // MOE_TM, N_EXPERTS, D_FF_EXPERT // MOE_TF),
        in_specs=[pl.BlockSpec((MOE_TM, D_MODEL), row, pipeline_mode=once),
                  pl.BlockSpec((MOE_TM, D_MODEL), row, pipeline_mode=once),
                  pl.BlockSpec((MOE_TM, LANES), row, pipeline_mode=once),
                  pl.BlockSpec((N_EXPERTS, MOE_TM), lambda t, e, c, cnt: (0, t), pipeline_mode=once),
                  pl.BlockSpec((MOE_TM, MOE_TM), lambda t, e, c, cnt: (0, 0), pipeline_mode=once),
                  pl.BlockSpec((1, D_MODEL, MOE_TF), lambda t, e, c, cnt: (e, 0, c)),
                  pl.BlockSpec((1, D_MODEL, MOE_TF), lambda t, e, c, cnt: (e, 0, c)),
                  pl.BlockSpec((1, MOE_TF, D_MODEL), lambda t, e, c, cnt: (e, c, 0))],
        out_specs=pl.BlockSpec((MOE_TM, D_MODEL), row),
        scratch_shapes=[pltpu.VMEM((MOE_TM, D_MODEL), F32),
                        pltpu.VMEM((N_EXPERTS, MOE_TM), F32),
                        pltpu.VMEM((MOE_TM, D_MODEL), BF16),
                        pltpu.VMEM((MOE_TM, D_MODEL), F32)],
    )
    return pl.pallas_call(
        _moe_kernel,
        grid_spec=grid_spec,
        out_shape=jax.ShapeDtypeStruct((N_TOK, D_MODEL), F32),
        compiler_params=_params(("arbitrary", "arbitrary", "arbitrary")),
        name="moe",
    )(counts, h1, n, gates, chosen_t, before, wg, wu, wd)


def _ple_kernel(h2_ref, p_ref, gple_ref, wproj_ref, wgate_ref, o_ref):
    o_ref[...] = _ple(h2_ref[...], p_ref, gple_ref, wproj_ref, wgate_ref)


def _ple_call(h2, p, gple, wproj, wgate):
    row = lambda t: (t, 0)
    return pl.pallas_call(
        _ple_kernel,
        grid=(NT,),
        in_specs=[pl.BlockSpec((TM, D_MODEL), row), pl.BlockSpec((TM, PLE_DIM), row),
                  _const_spec((1, D_MODEL)), _const_spec((PLE_DIM, D_MODEL)),
                  _const_spec((D_MODEL, D_MODEL))],
        out_specs=pl.BlockSpec((TM, D_MODEL), row),
        out_shape=jax.ShapeDtypeStruct((N_TOK, D_MODEL), F32),
        compiler_params=_params(("arbitrary",)),
        name="ple",
    )(h2, p, gple, wproj, wgate)


def _sample_mixing(layer, page_table, k_t, v_t, lf_t, q_tile, k_new, v_new, logf_new, gatt):
    tok = jnp.arange(PAGE_SIZE)
    upper = (tok[:, None] > tok[None, :]).astype(BF16)
    tri = (tok[:, None] <= tok[None, :]).astype(BF16)
    q_s = q_tile.transpose(1, 0, 2).reshape(DEC_BATCH, DEC_SEQ, N_HEADS, HEAD_DIM)
    eye = jnp.eye(N_HEADS, dtype=BF16)
    qblk = (q_s[:, :, :, None, :] * eye[None, None, :, :, None]).reshape(
        DEC_BATCH, DEC_SEQ * N_HEADS, ATT_WIDTH)
    lfn = jnp.pad(logf_new.reshape(DEC_BATCH, DEC_SEQ, N_HEADS).transpose(0, 2, 1),
                  ((0, 0), (0, 0), (0, LANES - DEC_SEQ)))
    return _attn_sample(layer, page_table, k_t, v_t, lf_t, qblk, lfn,
                        k_new.reshape(DEC_BATCH, DEC_SEQ, ATT_WIDTH), v_new.reshape(DEC_BATCH, DEC_SEQ, ATT_WIDTH),
                        upper, tri, gatt)


def kernel(x_prompt, x_sample, cache_k, cache_v, cache_logf, state_conv, page_table, p_prompt, p_sample, g_mix, w_in, b_forget, g_q, g_k, w_dw, b_dw, g_conv_ln, b_conv_ln, g_out_att, g_out_conv, w_out, g_ffn, w_ff_gate, w_ff_up, w_ff_down, w_router, w_ex_gate, w_ex_up, w_ex_down, g_ple, w_ple_proj, w_ple_gate):
    h = jnp.concatenate([x_prompt.reshape(N_PROMPT, D_MODEL), x_sample.reshape(N_SAMPLE, D_MODEL)], axis=0)
    k_t = jnp.transpose(cache_k, (0, 1, 3, 4, 2))
    v_t = jnp.transpose(cache_v, (0, 1, 3, 4, 2))
    lf_t = jnp.transpose(cache_logf, (0, 1, 3, 2))

    idx = jnp.arange(ATT_WIDTH)
    bd = jnp.where(idx[:, None] // HEAD_DIM == idx[None, :] // HEAD_DIM, 1.0 / HEAD_DIM, 0.0).astype(BF16)
    r = jnp.arange(TM)
    tri = (r[None, :] <= r[:, None]).astype(BF16)
    rm = jnp.arange(MOE_TM)
    before = (rm[:, None] < rm[None, :]).astype(BF16)

    states = []
    for i in range(DEPTH):
        w_pad = jnp.pad(w_in[i], ((0, 0), (0, IN_PAD - IN_WIDTH))).astype(BF16)
        bfp = jnp.pad(b_forget[i], (0, LANES - N_HEADS)).reshape(1, LANES)
        gq = (jnp.tile(g_q[i], N_HEADS) * ATT_SCALE).reshape(1, ATT_WIDTH)
        gk = jnp.tile(g_k[i], N_HEADS).reshape(1, ATT_WIDTH)
        q4, augq, k4, augk, vt, k, v, logf, u = _inproj(
            h, g_mix[i].reshape(1, D_MODEL), w_pad, bfp, gq, gk, bd, tri)
        gatt = g_out_att[i].reshape(1, ATT_WIDTH)

        attp = _attn_prompt(q4, augq, k4, augk.reshape(NT, TM, LANES), vt, gatt)

        atts = _sample_mixing(i, page_table, k_t, v_t, lf_t, q4[NT_PROMPT],
                              k[N_PROMPT:], v[N_PROMPT:], logf[N_PROMPT:], gatt)

        state_pad = jnp.pad(state_conv[i], ((0, 0), (HALO - (CONV_WIDTH - 1), 0), (0, 0)))
        w_dw_pad = jnp.pad(w_dw[i], ((0, HALO - CONV_WIDTH), (0, 0)))
        cn = _conv(u, state_pad, w_dw_pad, b_dw[i].reshape(1, CONV_CH), g_conv_ln[i].reshape(1, CONV_CH),
                   b_conv_ln[i].reshape(1, CONV_CH), g_out_conv[i].reshape(1, CONV_CH))

        wo = w_out[i].astype(BF16)
        p_all = jnp.concatenate([p_prompt[i].reshape(N_PROMPT, PLE_DIM), p_sample[i].reshape(N_SAMPLE, PLE_DIM)], axis=0)
        gple = g_ple[i].reshape(1, D_MODEL)
        wproj = w_ple_proj[i].astype(BF16)
        wgate = w_ple_gate[i].astype(BF16)
        gffn = g_ffn[i].reshape(1, D_MODEL)
        j = i // 2
        if i % 2 == 0:
            h1, n = _outproj(h, attp, atts, cn, wo[:ATT_WIDTH], wo[ATT_WIDTH:], gffn)
            h = _ffn_ple(h1, n, w_ff_gate[j].astype(BF16), w_ff_up[j].astype(BF16),
                         w_ff_down[j].astype(BF16), p_all, gple, wproj, wgate)
        else:
            wr = jnp.pad(w_router[j], ((0, 0), (0, LANES - N_EXPERTS))).astype(BF16)
            h1, n, gates, chosen, chosen_t = _outproj(h, attp, atts, cn, wo[:ATT_WIDTH], wo[ATT_WIDTH:], gffn,
                                                      router=wr)
            counts = jnp.sum(chosen[:, :N_EXPERTS].reshape(N_TOK // MOE_TM, MOE_TM, N_EXPERTS),
                             axis=1).astype(jnp.int32)
            h2 = _moe(counts, h1, n, gates, chosen_t, before, w_ex_gate[j].astype(BF16),
                      w_ex_up[j].astype(BF16), w_ex_down[j].astype(BF16))
            h = _ple_call(h2, p_all, gple, wproj, wgate)

        conv_p = u[:N_PROMPT].reshape(BATCH, SEQ, CONV_CH)[:, SEQ - (CONV_WIDTH - 1):]
        conv_s = jnp.concatenate([state_conv[i], u[N_PROMPT:].reshape(DEC_BATCH, DEC_SEQ, CONV_CH)],
                                 axis=1)[:, DEC_SEQ:]
        states.append((k, v, logf, conv_p, conv_s))

    def stack(idx, lo, hi, shape):
        return jnp.stack([s[idx][lo:hi].reshape(shape) for s in states])

    kv_p = (BATCH, SEQ, N_HEADS, HEAD_DIM)
    kv_s = (DEC_BATCH, DEC_SEQ, N_HEADS, HEAD_DIM)
    return (h[:N_PROMPT].reshape(BATCH, SEQ, D_MODEL),
            h[N_PROMPT:].reshape(DEC_BATCH, DEC_SEQ, D_MODEL),
            stack(0, 0, N_PROMPT, kv_p), stack(1, 0, N_PROMPT, kv_p),
            stack(2, 0, N_PROMPT, (BATCH, SEQ, N_HEADS)),
            jnp.stack([s[3] for s in states]),
            stack(0, N_PROMPT, N_TOK, kv_s), stack(1, N_PROMPT, N_TOK, kv_s),
            stack(2, N_PROMPT, N_TOK, (DEC_BATCH, DEC_SEQ, N_HEADS)),
            jnp.stack([s[4] for s in states]))
```

```python
import functools

import jax
import jax.numpy as jnp
from jax import lax
from jax.experimental import pallas as pl
from jax.experimental.pallas import tpu as pltpu

F32 = jnp.float32
BF16 = jnp.bfloat16

D_MODEL = 1024
BATCH = 2
SEQ = 8192
DEPTH = 4
DEC_BATCH = 32
DEC_SEQ = 8
PAST_LEN = 8192
PAGE_SIZE = 128
N_PAGES = PAST_LEN // PAGE_SIZE
N_HEADS = 8
HEAD_DIM = 64
ATT_WIDTH = N_HEADS * HEAD_DIM
CONV_CH = D_MODEL // 2
CONV_WIDTH = 31
IN_WIDTH = 3 * ATT_WIDTH + 2 * CONV_CH + N_HEADS
D_FF = 2816
N_EXPERTS = 8
D_FF_EXPERT = D_MODEL * 7 // 2
PLE_DIM = 256
EPS = 1e-6
ATT_SCALE = HEAD_DIM ** -0.5

LANES = 128
N_PROMPT = BATCH * SEQ
N_SAMPLE = DEC_BATCH * DEC_SEQ
N_TOK = N_PROMPT + N_SAMPLE
TM = 256
NT = N_TOK // TM
NT_PROMPT = N_PROMPT // TM
TILES_PER_SEQ = SEQ // TM
IN_PAD = 3 * ATT_WIDTH + 2 * CONV_CH + LANES
VT_ROWS = HEAD_DIM + 16
HALO = 32
PAGES_PER_STEP = 8
KEYS_PER_STEP = PAGES_PER_STEP * PAGE_SIZE
STEPS_PER_SEQ = N_PAGES // PAGES_PER_STEP
MOE_TM = 1280
MOE_TF = 896
MOE_CH = 128
NEG = -1e30
VMEM_LIMIT = 56 * 1024 * 1024


def _params(sem, vmem=VMEM_LIMIT):
    return pltpu.CompilerParams(dimension_semantics=sem, vmem_limit_bytes=vmem)


def _const_spec(shape):
    nd = len(shape)
    return pl.BlockSpec(shape, lambda *_: (0,) * nd, pipeline_mode=pl.Buffered(1))


def _split3(x):
    hi = x.astype(BF16)
    r = x - hi.astype(F32)
    mid = r.astype(BF16)
    lo = (r - mid.astype(F32)).astype(BF16)
    return hi, mid, lo


def _rms(x, g):
    return x * lax.rsqrt(jnp.mean(x * x, axis=-1, keepdims=True) + EPS) * g


def _sigmoid(x):
    return 1.0 / (1.0 + jnp.exp(-x))


def _silu(x):
    return x * _sigmoid(x)


def _log_sigmoid(x):
    return jnp.minimum(x, 0.0) - jnp.log1p(jnp.exp(-jnp.abs(x)))


def _inproj_kernel(h_ref, gmix_ref, w_ref, bf_ref, gq_ref, gk_ref, bd_ref, tri_ref,
                   q4_ref, augq_ref, k4_ref, augk_ref, vt_ref, k_ref, v_ref, logf_ref, u_ref,
                   carry_ref):
    i = pl.program_id(0)
    xn = _rms(h_ref[...], gmix_ref[...]).astype(BF16)

    def proj(c0, c1):
        return jnp.dot(xn, w_ref[:, c0:c1], preferred_element_type=F32)

    def head_norm(z, g):
        ms = jnp.dot((z * z).astype(BF16), bd_ref[...], preferred_element_type=F32)
        return z * lax.rsqrt(ms + EPS) * g

    q = head_norm(proj(0, 512), gq_ref[...])
    k = head_norm(proj(512, 1024), gk_ref[...])
    v = proj(1024, 1536)
    ua = proj(1536, 2048)
    ug = proj(2048, 2560)
    zf = proj(2560, IN_PAD)

    lane = lax.broadcasted_iota(jnp.int32, (TM, LANES), 1)
    logf = jnp.where(lane < N_HEADS, _log_sigmoid(zf + bf_ref[...]), 0.0)

    k_ref[...] = k
    v_ref[...] = v
    u_ref[...] = ua * _sigmoid(ug)
    logf_ref[...] = logf[:, :N_HEADS]

    qb = q.astype(BF16)
    kb = k.astype(BF16)
    for j in range(4):
        q4_ref[0, j] = qb[:, j * LANES:(j + 1) * LANES]
        k4_ref[0, j] = kb[:, j * LANES:(j + 1) * LANES]
    vt_ref[0, :, 0:HEAD_DIM, :] = v.T.astype(BF16).reshape(N_HEADS, HEAD_DIM, TM)
    vt_ref[0, :, HEAD_DIM:VT_ROWS, :] = jnp.ones((N_HEADS, VT_ROWS - HEAD_DIM, TM), BF16)

    hi, mid, lo = _split3(logf)
    tri = tri_ref[...]
    c = (jnp.dot(tri, hi, preferred_element_type=F32)
         + jnp.dot(tri, mid, preferred_element_type=F32)
         + jnp.dot(tri, lo, preferred_element_type=F32))
    carry = jnp.where(i % TILES_PER_SEQ == 0, 0.0, carry_ref[0:1, :])
    c = c + carry
    carry_ref[...] = jnp.broadcast_to(c[TM - 1:TM, :], (8, LANES))

    chi, cmid, clo = (p.astype(F32) for p in _split3(c))
    ones_q = jnp.where((lane >= 24) & (lane < 48), 1.0, 0.0)
    ones_k = jnp.where(lane < 24, 1.0, 0.0)
    augq = chi + pltpu.roll(cmid, 8, 1) + pltpu.roll(clo, 16, 1) + ones_q
    augk = ones_k - pltpu.roll(chi, 24, 1) - pltpu.roll(cmid, 32, 1) - pltpu.roll(clo, 40, 1)
    augq_ref[...] = augq.astype(BF16)
    augk_ref[...] = augk.astype(BF16)


def _inproj(h, gmix, w, bfp, gq, gk, bd, tri):
    row = lambda i: (i, 0)
    out_shape = (
        jax.ShapeDtypeStruct((NT, 4, TM, LANES), BF16),
        jax.ShapeDtypeStruct((N_TOK, LANES), BF16),
        jax.ShapeDtypeStruct((NT, 4, TM, LANES), BF16),
        jax.ShapeDtypeStruct((N_TOK, LANES), BF16),
        jax.ShapeDtypeStruct((NT, N_HEADS, VT_ROWS, TM), BF16),
        jax.ShapeDtypeStruct((N_TOK, ATT_WIDTH), F32),
        jax.ShapeDtypeStruct((N_TOK, ATT_WIDTH), F32),
        jax.ShapeDtypeStruct((N_TOK, N_HEADS), F32),
        jax.ShapeDtypeStruct((N_TOK, CONV_CH), F32),
    )
    out_specs = (
        pl.BlockSpec((1, 4, TM, LANES), lambda i: (i, 0, 0, 0)),
        pl.BlockSpec((TM, LANES), row),
        pl.BlockSpec((1, 4, TM, LANES), lambda i: (i, 0, 0, 0)),
        pl.BlockSpec((TM, LANES), row),
        pl.BlockSpec((1, N_HEADS, VT_ROWS, TM), lambda i: (i, 0, 0, 0)),
        pl.BlockSpec((TM, ATT_WIDTH), row),
        pl.BlockSpec((TM, ATT_WIDTH), row),
        pl.BlockSpec((TM, N_HEADS), row),
        pl.BlockSpec((TM, CONV_CH), row),
    )
    return pl.pallas_call(
        _inproj_kernel,
        grid=(NT,),
        in_specs=[pl.BlockSpec((TM, D_MODEL), row),
                  _const_spec((1, D_MODEL)), _const_spec((D_MODEL, IN_PAD)), _const_spec((1, LANES)),
                  _const_spec((1, ATT_WIDTH)), _const_spec((1, ATT_WIDTH)),
                  _const_spec((ATT_WIDTH, ATT_WIDTH)), _const_spec((TM, TM))],
        out_specs=out_specs,
        out_shape=out_shape,
        scratch_shapes=[pltpu.VMEM((8, LANES), F32)],
        compiler_params=_params(("arbitrary",)),
        name="inproj",
    )(h, gmix, w, bfp, gq, gk, bd, tri)


def _attn_kernel(q4_ref, augq_ref, k4_ref, augk_ref, vt_ref, gatt_ref, o_ref, qa_scr, m_scr, acc_scr, st_scr):
    qi = pl.program_id(1)
    lane = lax.broadcasted_iota(jnp.int32, (TM, LANES), 1)
    augq = augq_ref[...]
    zero = jnp.zeros((), BF16)
    for h in range(N_HEADS):
        qm = jnp.where((lane // HEAD_DIM) == (h % 2), q4_ref[0, h // 2], zero)
        am = jnp.where(((lane % 8) == h) & (lane < 48), augq, zero)
        qa_scr[h] = jnp.concatenate([qm, am], axis=1)
    m_scr[...] = jnp.full((N_HEADS, 1, TM), NEG, F32)
    acc_scr[...] = jnp.zeros((N_HEADS, VT_ROWS, TM), F32)

    key_pos = lax.broadcasted_iota(jnp.int32, (TM, TM), 0)
    qry_pos = lax.broadcasted_iota(jnp.int32, (TM, TM), 1)
    causal = key_pos <= qry_pos

    def scores(kb, slot, diagonal):
        aug = augk_ref[kb]
        for h in range(N_HEADS):
            ka = jnp.concatenate([k4_ref[kb, h // 2], aug], axis=1)
            st = lax.dot_general(ka, qa_scr[h], (((1,), (1,)), ((), ())),
                                 preferred_element_type=F32)
            st_scr[slot, h] = jnp.where(causal, st, NEG) if diagonal else st

    def softmax_pv(kb, slot):
        for h in range(N_HEADS):
            m = m_scr[h]
            m_new = jnp.maximum(m, jnp.max(st_scr[slot, h], axis=0, keepdims=True))
            p = jnp.exp(st_scr[slot, h] - m_new).astype(BF16)
            pv = jnp.dot(vt_ref[kb, h], p, preferred_element_type=F32)
            acc_scr[h] = jnp.exp(m - m_new) * acc_scr[h] + pv
            m_scr[h] = m_new

    @pl.when(qi == 0)
    def _():
        scores(0, 0, True)
        softmax_pv(0, 0)

    @pl.when(qi > 0)
    def _():
        scores(0, 0, False)

    def body(kk, carry):
        kb = 2 * kk
        scores(kb + 1, 1, False)
        softmax_pv(kb, 0)
        scores(kb + 2, 0, False)
        softmax_pv(kb + 1, 1)
        return carry

    lax.fori_loop(0, jnp.maximum(qi - 1, 0) // 2, body, 0)

    @pl.when(qi % 2 == 1)
    def _():
        scores(qi, 1, True)
        softmax_pv(qi - 1, 0)
        softmax_pv(qi, 1)

    @pl.when((qi % 2 == 0) & (qi > 0))
    def _():
        scores(qi - 1, 1, False)
        softmax_pv(qi - 2, 0)
        scores(qi, 0, True)
        softmax_pv(qi - 1, 1)
        softmax_pv(qi, 0)

    acc = acc_scr[...]
    ot = acc[:, 0:HEAD_DIM, :] / acc[:, HEAD_DIM:HEAD_DIM + 1, :]
    o = ot.reshape(ATT_WIDTH, TM).T
    o_ref[...] = _rms(o, gatt_ref[...]).astype(BF16)


def _attn_prompt(q4, augq, k4, augk3, vt, gatt):
    seq_blk = lambda b, qi: (b, 0, 0, 0)
    once = pl.Buffered(1)
    return pl.pallas_call(
        _attn_kernel,
        grid=(BATCH, TILES_PER_SEQ),
        in_specs=[pl.BlockSpec((1, 4, TM, LANES), lambda b, qi: (b * TILES_PER_SEQ + qi, 0, 0, 0)),
                  pl.BlockSpec((TM, LANES), lambda b, qi: (b * TILES_PER_SEQ + qi, 0)),
                  pl.BlockSpec((TILES_PER_SEQ, 4, TM, LANES), seq_blk, pipeline_mode=once),
                  pl.BlockSpec((TILES_PER_SEQ, TM, LANES), lambda b, qi: (b, 0, 0), pipeline_mode=once),
                  pl.BlockSpec((TILES_PER_SEQ, N_HEADS, VT_ROWS, TM), seq_blk, pipeline_mode=once),
                  _const_spec((1, ATT_WIDTH))],
        out_specs=pl.BlockSpec((TM, ATT_WIDTH), lambda b, qi: (b * TILES_PER_SEQ + qi, 0)),
        out_shape=jax.ShapeDtypeStruct((N_PROMPT, ATT_WIDTH), BF16),
        scratch_shapes=[pltpu.VMEM((N_HEADS, TM, 2 * LANES), BF16),
                        pltpu.VMEM((N_HEADS, 1, TM), F32),
                        pltpu.VMEM((N_HEADS, VT_ROWS, TM), F32),
                        pltpu.VMEM((2, N_HEADS, TM, TM), F32)],
        compiler_params=_params(("arbitrary", "arbitrary")),
        name="attn_prompt",
    )(q4, augq, k4, augk3, vt, gatt)


def _attn_sample_kernel(pt_ref, *refs):
    del pt_ref
    n = PAGES_PER_STEP
    kp, vp, fp = refs[:n], refs[n:2 * n], refs[2 * n:3 * n]
    (qblk_ref, lfn_ref, knew_ref, vnew_ref, upper_ref, tri_ref, gatt_ref,
     o_ref, m_scr, l_scr, acc_scr, cb_scr, carry_scr) = refs[3 * n:]
    s = pl.program_id(1)
    rows = DEC_SEQ * N_HEADS
    qblk = qblk_ref[0]
    row = lax.broadcasted_iota(jnp.int32, (rows, LANES), 0)
    lane = lax.broadcasted_iota(jnp.int32, (rows, LANES), 1)
    contract_last = (((1,), (1,)), ((), ()))

    def over_tokens(x, mat_ref):
        return sum(jnp.dot(p, mat_ref[...], preferred_element_type=F32) for p in _split3(x))

    @pl.when(s == 0)
    def _():
        cn_t = over_tokens(lfn_ref[0], tri_ref)
        cn_rows = jnp.concatenate([cn_t] * DEC_SEQ, axis=0)
        cb = jnp.sum(jnp.where(lane == row // N_HEADS, cn_rows, 0.0), axis=-1, keepdims=True)
        cb_scr[...] = jnp.broadcast_to(cb, (rows, LANES))
        carry_scr[...] = jnp.zeros((N_HEADS, LANES), F32)
        pad = jnp.zeros((LANES - DEC_SEQ, ATT_WIDTH), BF16)
        kn = jnp.concatenate([knew_ref[0].astype(BF16), pad], axis=0)
        vn = jnp.concatenate([vnew_ref[0].astype(BF16), pad], axis=0)
        st = lax.dot_general(qblk, kn, contract_last, preferred_element_type=F32)
        st = jnp.where((lane < DEC_SEQ) & (lane <= row // N_HEADS), st + cb - cn_rows, NEG)
        m = jnp.max(st, axis=-1, keepdims=True)
        p = jnp.exp(st - m)
        m_scr[...] = jnp.broadcast_to(m, (rows, LANES))
        l_scr[...] = jnp.broadcast_to(jnp.sum(p, axis=-1, keepdims=True), (rows, LANES))
        acc_scr[...] = jnp.dot(p.astype(BF16), vn, preferred_element_type=F32)

    lf = jnp.concatenate([r[0, 0] for r in fp], axis=0)
    inside = over_tokens(lf, upper_ref)
    total = jnp.sum(lf, axis=-1, keepdims=True)
    carry = carry_scr[...]
    bias = []
    for j in range(n):
        d = inside[j * N_HEADS:(j + 1) * N_HEADS] + carry
        carry = carry + total[j * N_HEADS:(j + 1) * N_HEADS]
        bias.append(jnp.concatenate([d] * DEC_SEQ, axis=0))
    carry_scr[...] = carry
    cb = cb_scr[...]
    bias = jnp.concatenate([b + cb for b in bias], axis=1)

    kt = jnp.concatenate([r[0, 0].reshape(ATT_WIDTH, PAGE_SIZE).astype(BF16) for r in kp], axis=1)
    st = jnp.dot(qblk, kt, preferred_element_type=F32) + bias
    m_old = m_scr[:, 0:1]
    m_new = jnp.maximum(m_old, jnp.max(st, axis=-1, keepdims=True))
    alpha = jnp.exp(m_old - m_new)
    p = jnp.exp(st - m_new)
    l_new = alpha * l_scr[:, 0:1] + jnp.sum(p, axis=-1, keepdims=True)
    vt = jnp.concatenate([r[0, 0].reshape(ATT_WIDTH, PAGE_SIZE).astype(BF16) for r in vp], axis=1)
    pv = lax.dot_general(p.astype(BF16), vt, contract_last, preferred_element_type=F32)
    acc = alpha * acc_scr[...] + pv
    m_scr[...] = jnp.broadcast_to(m_new, (rows, LANES))
    l_scr[...] = jnp.broadcast_to(l_new, (rows, LANES))
    acc_scr[...] = acc

    @pl.when(s == STEPS_PER_SEQ - 1)
    def _():
        o = acc / l_new
        r2 = lax.broadcasted_iota(jnp.int32, (rows, ATT_WIDTH), 0)
        c2 = lax.broadcasted_iota(jnp.int32, (rows, ATT_WIDTH), 1)
        o = jnp.where(c2 // HEAD_DIM == r2 % N_HEADS, o, 0.0)
        att = jnp.sum(o.reshape(DEC_SEQ, N_HEADS, ATT_WIDTH), axis=1)
        o_ref[...] = _rms(att, gatt_ref[...])


def _attn_sample(layer, page_table, k_t, v_t, lf_t, qblk, lfn, knew, vnew, upper, tri, gatt):
    def page(j):
        return lambda b, s, pt: pt[b, N_PAGES - 1 - (s * PAGES_PER_STEP + j)]

    def kv_spec(j):
        return pl.BlockSpec((1, 1, N_HEADS, HEAD_DIM, PAGE_SIZE),
                            lambda b, s, pt: (layer, page(j)(b, s, pt), 0, 0, 0))

    def lf_spec(j):
        return pl.BlockSpec((1, 1, N_HEADS, PAGE_SIZE), lambda b, s, pt: (layer, page(j)(b, s, pt), 0, 0))

    rows = DEC_SEQ * N_HEADS
    per_seq = lambda shape: pl.BlockSpec((1,) + shape, lambda b, s, pt: (b, 0, 0))
    const = lambda shape: pl.BlockSpec(shape, lambda b, s, pt: (0, 0))
    pages = range(PAGES_PER_STEP)
    grid_spec = pltpu.PrefetchScalarGridSpec(
        num_scalar_prefetch=1,
        grid=(DEC_BATCH, STEPS_PER_SEQ),
        in_specs=([kv_spec(j) for j in pages] + [kv_spec(j) for j in pages] + [lf_spec(j) for j in pages]
                  + [per_seq((rows, ATT_WIDTH)), per_seq((N_HEADS, LANES)),
                     per_seq((DEC_SEQ, ATT_WIDTH)), per_seq((DEC_SEQ, ATT_WIDTH)),
                     const((PAGE_SIZE, PAGE_SIZE)), const((LANES, LANES)), const((1, ATT_WIDTH))]),
        out_specs=pl.BlockSpec((DEC_SEQ, ATT_WIDTH), lambda b, s, pt: (b, 0)),
        scratch_shapes=[pltpu.VMEM((rows, LANES), F32), pltpu.VMEM((rows, LANES), F32),
                        pltpu.VMEM((rows, ATT_WIDTH), F32), pltpu.VMEM((rows, LANES), F32),
                        pltpu.VMEM((N_HEADS, LANES), F32)],
    )
    return pl.pallas_call(
        _attn_sample_kernel,
        grid_spec=grid_spec,
        out_shape=jax.ShapeDtypeStruct((N_SAMPLE, ATT_WIDTH), F32),
        compiler_params=_params(("arbitrary", "arbitrary")),
        name="attn_sample",
    )(page_table, *([k_t] * PAGES_PER_STEP), *([v_t] * PAGES_PER_STEP), *([lf_t] * PAGES_PER_STEP),
      qblk, lfn, knew, vnew, upper, tri, gatt)


def _conv_kernel(u_ref, prev_ref, state_ref, w_ref, bdw_ref, gln_ref, bln_ref, gcv_ref,
                 o_ref, xs_scr, xs3_scr, y_scr):
    t = pl.program_id(0)

    @pl.when(t < NT_PROMPT)
    def _():
        first = (t % TILES_PER_SEQ) == 0
        xs_scr[0:HALO, :] = jnp.where(first, 0.0, prev_ref[...])
        xs_scr[HALO:, :] = u_ref[...]
        acc = jnp.zeros((TM, CONV_CH), F32)
        for w in range(CONV_WIDTH):
            off = w + HALO - (CONV_WIDTH - 1)
            acc = acc + xs_scr[off:off + TM, :] * w_ref[w:w + 1, :]
        y_scr[...] = acc

    @pl.when(t == NT_PROMPT)
    def _():
        xs3_scr[:, 0:HALO, :] = state_ref[...]
        xs3_scr[:, HALO:, :] = u_ref[...].reshape(DEC_BATCH, DEC_SEQ, CONV_CH)
        acc = jnp.zeros((DEC_BATCH, DEC_SEQ, CONV_CH), F32)
        for w in range(CONV_WIDTH):
            off = w + HALO - (CONV_WIDTH - 1)
            acc = acc + xs3_scr[:, off:off + DEC_SEQ, :] * w_ref[w:w + 1, :]
        y_scr[...] = acc.reshape(TM, CONV_CH)

    y = y_scr[...] + bdw_ref[...]
    yc = y - jnp.mean(y, axis=-1, keepdims=True)
    yn = yc * lax.rsqrt(jnp.mean(yc * yc, axis=-1, keepdims=True) + EPS) * gln_ref[...] + bln_ref[...]
    o_ref[...] = _rms(_silu(yn), gcv_ref[...]).astype(BF16)


def _conv(u, state_pad, w, bdw, gln, bln, gcv):
    halo_blocks = TM // HALO
    return pl.pallas_call(
        _conv_kernel,
        grid=(NT,),
        in_specs=[pl.BlockSpec((TM, CONV_CH), lambda t: (t, 0)),
                  pl.BlockSpec((HALO, CONV_CH), lambda t: (jnp.maximum(t * halo_blocks - 1, 0), 0)),
                  _const_spec((DEC_BATCH, HALO, CONV_CH)),
                  _const_spec((HALO, CONV_CH)),
                  _const_spec((1, CONV_CH)), _const_spec((1, CONV_CH)),
                  _const_spec((1, CONV_CH)), _const_spec((1, CONV_CH))],
        out_specs=pl.BlockSpec((TM, CONV_CH), lambda t: (t, 0)),
        out_shape=jax.ShapeDtypeStruct((N_TOK, CONV_CH), BF16),
        scratch_shapes=[pltpu.VMEM((TM + HALO, CONV_CH), F32),
                        pltpu.VMEM((DEC_BATCH, HALO + DEC_SEQ, CONV_CH), F32),
                        pltpu.VMEM((TM, CONV_CH), F32)],
        compiler_params=_params(("arbitrary",)),
        name="conv",
    )(u, u, state_pad, w, bdw, gln, bln, gcv)


def _top2_gates(logits):
    lane = lax.broadcasted_iota(jnp.int32, logits.shape, 1)
    m1 = jnp.max(logits, axis=-1, keepdims=True)
    i1 = jnp.min(jnp.where(logits == m1, lane, LANES), axis=-1, keepdims=True)
    rest = jnp.where(lane == i1, NEG, logits)
    m2 = jnp.max(rest, axis=-1, keepdims=True)
    i2 = jnp.min(jnp.where(rest == m2, lane, LANES), axis=-1, keepdims=True)
    e = jnp.exp(m2 - m1)
    w1 = 1.0 / (1.0 + e)
    w2 = e / (1.0 + e)
    gates = jnp.where(lane == i1, w1, 0.0) + jnp.where(lane == i2, w2, 0.0)
    chosen = jnp.where((lane == i1) | (lane == i2), 1.0, 0.0)
    return gates, chosen


def _outproj_kernel(*refs, with_router):
    if with_router:
        (h_ref, attp_ref, atts_ref, cn_ref, wa_ref, wc_ref, gffn_ref, wr_ref,
         h1_ref, n_ref, gates_ref, chosen_ref, chosen_t_ref) = refs
    else:
        h_ref, attp_ref, atts_ref, cn_ref, wa_ref, wc_ref, gffn_ref, h1_ref, n_ref = refs
    t = pl.program_id(0)
    att = jnp.where(t == NT_PROMPT, atts_ref[...].astype(BF16), attp_ref[...])
    h1 = (h_ref[...]
          + jnp.dot(att, wa_ref[...], preferred_element_type=F32)
          + jnp.dot(cn_ref[...], wc_ref[...], preferred_element_type=F32))
    h1_ref[...] = h1
    n = _rms(h1, gffn_ref[...])
    n_ref[...] = n.astype(BF16)
    if with_router:
        logits = jnp.dot(n.astype(BF16), wr_ref[...], preferred_element_type=F32)
        lane = lax.broadcasted_iota(jnp.int32, logits.shape, 1)
        gates, chosen = _top2_gates(jnp.where(lane < N_EXPERTS, logits, NEG))
        gates_ref[...] = gates
        chosen_ref[...] = chosen
        chosen_t_ref[...] = chosen.T[:N_EXPERTS]


def _outproj(h, attp, atts, cn, wa, wc, gffn, router=None):
    row = lambda t: (t, 0)
    in_specs = [pl.BlockSpec((TM, D_MODEL), row),
                pl.BlockSpec((TM, ATT_WIDTH), lambda t: (jnp.minimum(t, NT_PROMPT - 1), 0)),
                _const_spec((N_SAMPLE, ATT_WIDTH)),
                pl.BlockSpec((TM, CONV_CH), row),
                _const_spec((ATT_WIDTH, D_MODEL)), _const_spec((CONV_CH, D_MODEL)),
                _const_spec((1, D_MODEL))]
    out_specs = [pl.BlockSpec((TM, D_MODEL), row), pl.BlockSpec((TM, D_MODEL), row)]
    out_shape = [jax.ShapeDtypeStruct((N_TOK, D_MODEL), F32), jax.ShapeDtypeStruct((N_TOK, D_MODEL), BF16)]
    args = [h, attp, atts, cn, wa, wc, gffn]
    if router is not None:
        in_specs += [_const_spec((D_MODEL, LANES))]
        out_specs += [pl.BlockSpec((TM, LANES), row), pl.BlockSpec((TM, LANES), row),
                      pl.BlockSpec((N_EXPERTS, TM), lambda t: (0, t))]
        out_shape += [jax.ShapeDtypeStruct((N_TOK, LANES), F32), jax.ShapeDtypeStruct((N_TOK, LANES), F32),
                      jax.ShapeDtypeStruct((N_EXPERTS, N_TOK), F32)]
        args.append(router)
    return pl.pallas_call(
        functools.partial(_outproj_kernel, with_router=router is not None),
        grid=(NT,),
        in_specs=in_specs,
        out_specs=tuple(out_specs),
        out_shape=tuple(out_shape),
        compiler_params=_params(("arbitrary",)),
        name="outproj",
    )(*args)


def _ple(h2, p_ref, gple_ref, wproj_ref, wgate_ref):
    gate = _sigmoid(jnp.dot(_rms(h2, gple_ref[...]).astype(BF16), wgate_ref[...],
                            preferred_element_type=F32))
    proj = jnp.dot(p_ref[...].astype(BF16), wproj_ref[...], preferred_element_type=F32)
    return h2 + proj * gate


def _ffn_ple_kernel(h1_ref, n_ref, wg_ref, wu_ref, wd_ref, p_ref, gple_ref, wproj_ref, wgate_ref, o_ref):
    n = n_ref[...]
    g = jnp.dot(n, wg_ref[...], preferred_element_type=F32)
    u = jnp.dot(n, wu_ref[...], preferred_element_type=F32)
    a = (_silu(g) * u).astype(BF16)
    h2 = h1_ref[...] + jnp.dot(a, wd_ref[...], preferred_element_type=F32)
    o_ref[...] = _ple(h2, p_ref, gple_ref, wproj_ref, wgate_ref)


def _ffn_ple(h1, n, wg, wu, wd, p, gple, wproj, wgate):
    row = lambda t: (t, 0)
    return pl.pallas_call(
        _ffn_ple_kernel,
        grid=(NT,),
        in_specs=[pl.BlockSpec((TM, D_MODEL), row), pl.BlockSpec((TM, D_MODEL), row),
                  _const_spec((D_MODEL, D_FF)), _const_spec((D_MODEL, D_FF)), _const_spec((D_FF, D_MODEL)),
                  pl.BlockSpec((TM, PLE_DIM), row), _const_spec((1, D_MODEL)),
                  _const_spec((PLE_DIM, D_MODEL)), _const_spec((D_MODEL, D_MODEL))],
        out_specs=pl.BlockSpec((TM, D_MODEL), row),
        out_shape=jax.ShapeDtypeStruct((N_TOK, D_MODEL), F32),
        compiler_params=_params(("arbitrary",)),
        name="ffn_ple",
    )(h1, n, wg, wu, wd, p, gple, wproj, wgate)


def _moe_kernel(cnt_ref, h1_ref, n_ref, gates_ref, chosen_t_ref, before_ref, wg_ref, wu_ref, wd_ref,
                o_ref, acc_scr, slot_scr, xe_scr, ye_scr):
    t = pl.program_id(0)
    e = pl.program_id(1)
    c = pl.program_id(2)
    last_c = pl.num_programs(2) - 1
    n_chunks = (cnt_ref[t, e] + MOE_CH - 1) // MOE_CH

    @pl.when((e == 0) & (c == 0))
    def _():
        acc_scr[...] = h1_ref[...]
        chosen_t = chosen_t_ref[...]
        before = jnp.dot(chosen_t.astype(BF16), before_ref[...], preferred_element_type=F32)
        slot_scr[...] = jnp.where(chosen_t > 0.5, before, -1.0)

    def selection(j):
        slot = slot_scr[pl.ds(e, 1), :]
        row = (lax.broadcasted_iota(jnp.int32, (MOE_CH, MOE_TM), 0) + j * MOE_CH).astype(F32)
        return jnp.where(slot == row, 1.0, 0.0).astype(BF16)

    @pl.when(c == 0)
    def _():
        def gather(j, carry):
            r0 = pl.multiple_of(j * MOE_CH, MOE_CH)
            xe_scr[pl.ds(r0, MOE_CH), :] = jnp.dot(
                selection(j), n_ref[...], preferred_element_type=F32).astype(BF16)
            return carry
        lax.fori_loop(0, n_chunks, gather, 0)

    def expert(j, carry):
        r0 = pl.multiple_of(j * MOE_CH, MOE_CH)
        x = xe_scr[pl.ds(r0, MOE_CH), :]
        g = jnp.dot(x, wg_ref[0], preferred_element_type=F32)
        u = jnp.dot(x, wu_ref[0], preferred_element_type=F32)
        y = jnp.dot((_silu(g) * u).astype(BF16), wd_ref[0], preferred_element_type=F32)

        @pl.when(c == 0)
        def _():
            ye_scr[pl.ds(r0, MOE_CH), :] = y

        @pl.when(c > 0)
        def _():
            ye_scr[pl.ds(r0, MOE_CH), :] += y
        return carry

    lax.fori_loop(0, n_chunks, expert, 0)

    @pl.when(c == last_c)
    def _():
        lane = lax.broadcasted_iota(jnp.int32, (MOE_TM, LANES), 1)
        gate = jnp.sum(jnp.where(lane == e, gates_ref[...], 0.0), axis=-1, keepdims=True)

        def scatter(j, carry):
            r0 = pl.multiple_of(j * MOE_CH, MOE_CH)
            y = ye_scr[pl.ds(r0, MOE_CH), :]
            y_hi = y.astype(BF16)
            y_lo = (y - y_hi.astype(F32)).astype(BF16)
            sel = selection(j)
            back = lax.dot_general(jnp.concatenate([sel, sel], axis=0), jnp.concatenate([y_hi, y_lo], axis=0),
                                   (((0,), (0,)), ((), ())), preferred_element_type=F32)
            acc_scr[...] += gate * back
            return carry
        lax.fori_loop(0, n_chunks, scatter, 0)

    @pl.when((e == N_EXPERTS - 1) & (c == last_c))
    def _():
        o_ref[...] = acc_scr[...]


def _moe(counts, h1, n, gates, chosen_t, before, wg, wu, wd):
    row = lambda t, e, c, cnt: (t, 0)
    once = pl.Buffered(1)
    grid_spec = pltpu.PrefetchScalarGridSpec(
        num_scalar_prefetch=1,
        grid=(N_TOK // MOE_TM, N_EXPERTS, D_FF_EXPERT // MOE_TF),
        in_specs=[pl.BlockSpec((MOE_TM, D_MODEL), row, pipeline_mode=once),
                  pl.BlockSpec((MOE_TM, D_MODEL), row, pipeline_mode=once),
                  pl.BlockSpec((MOE_TM, LANES), row, pipeline_mode=once),
                  pl.BlockSpec((N_EXPERTS, MOE_TM), lambda t, e, c, cnt: (0, t), pipeline_mode=once),
                  pl.BlockSpec((MOE_TM, MOE_TM), lambda t, e, c, cnt: (0, 0), pipeline_mode=once),
                  pl.BlockSpec((1, D_MODEL, MOE_TF), lambda t, e, c, cnt: (e, 0, c)),
                  pl.BlockSpec((1, D_MODEL, MOE_TF), lambda t, e, c, cnt: (e, 0, c)),
                  pl.BlockSpec((1, MOE_TF, D_MODEL), lambda t, e, c, cnt: (e, c, 0))],
        out_specs=pl.BlockSpec((MOE_TM, D_MODEL), row),
        scratch_shapes=[pltpu.VMEM((MOE_TM, D_MODEL), F32),
                        pltpu.VMEM((N_EXPERTS, MOE_TM), F32),
                        pltpu.VMEM((MOE_TM, D_MODEL), BF16),
                        pltpu.VMEM((MOE_TM, D_MODEL), F32)],
    )
    return pl.pallas_call(
        _moe_kernel,
        grid_spec=grid_spec,
        out_shape=jax.ShapeDtypeStruct((N_TOK, D_MODEL), F32),
        compiler_params=_params(("arbitrary", "arbitrary", "arbitrary")),
        name="moe",
    )(counts, h1, n, gates, chosen_t, before, wg, wu, wd)


def _ple_kernel(h2_ref, p_ref, gple_ref, wproj_ref, wgate_ref, o_ref):
    o_ref[...] = _ple(h2_ref[...], p_ref, gple_ref, wproj_ref, wgate_ref)


def _ple_call(h2, p, gple, wproj, wgate):
    row = lambda t: (t, 0)
    return pl.pallas_call(
        _ple_kernel,
        grid=(NT,),
        in_specs=[pl.BlockSpec((TM, D_MODEL), row), pl.BlockSpec((TM, PLE_DIM), row),
                  _const_spec((1, D_MODEL)), _const_spec((PLE_DIM, D_MODEL)),
                  _const_spec((D_MODEL, D_MODEL))],
        out_specs=pl.BlockSpec((TM, D_MODEL), row),
        out_shape=jax.ShapeDtypeStruct((N_TOK, D_MODEL), F32),
        compiler_params=_params(("arbitrary",)),
        name="ple",
    )(h2, p, gple, wproj, wgate)


def _sample_mixing(layer, page_table, k_t, v_t, lf_t, q_tile, k_new, v_new, logf_new, gatt):
    tok = jnp.arange(PAGE_SIZE)
    upper = (tok[:, None] > tok[None, :]).astype(BF16)
    tri = (tok[:, None] <= tok[None, :]).astype(BF16)
    q_s = q_tile.transpose(1, 0, 2).reshape(DEC_BATCH, DEC_SEQ, N_HEADS, HEAD_DIM)
    eye = jnp.eye(N_HEADS, dtype=BF16)
    qblk = (q_s[:, :, :, None, :] * eye[None, None, :, :, None]).reshape(
        DEC_BATCH, DEC_SEQ * N_HEADS, ATT_WIDTH)
    lfn = jnp.pad(logf_new.reshape(DEC_BATCH, DEC_SEQ, N_HEADS).transpose(0, 2, 1),
                  ((0, 0), (0, 0), (0, LANES - DEC_SEQ)))
    return _attn_sample(layer, page_table, k_t, v_t, lf_t, qblk, lfn,
                        k_new.reshape(DEC_BATCH, DEC_SEQ, ATT_WIDTH), v_new.reshape(DEC_BATCH, DEC_SEQ, ATT_WIDTH),
                        upper, tri, gatt)


def kernel(x_prompt, x_sample, cache_k, cache_v, cache_logf, state_conv, page_table, p_prompt, p_sample, g_mix, w_in, b_forget, g_q, g_k, w_dw, b_dw, g_conv_ln, b_conv_ln, g_out_att, g_out_conv, w_out, g_ffn, w_ff_gate, w_ff_up, w_ff_down, w_router, w_ex_gate, w_ex_up, w_ex_down, g_ple, w_ple_proj, w_ple_gate):
    h = jnp.concatenate([x_prompt.reshape(N_PROMPT, D_MODEL), x_sample.reshape(N_SAMPLE, D_MODEL)], axis=0)
    k_t = jnp.transpose(cache_k, (0, 1, 3, 4, 2))
    v_t = jnp.transpose(cache_v, (0, 1, 3, 4, 2))
    lf_t = jnp.transpose(cache_logf, (0, 1, 3, 2))

    idx = jnp.arange(ATT_WIDTH)
    bd = jnp.where(idx[:, None] // HEAD_DIM == idx[None, :] // HEAD_DIM, 1.0 / HEAD_DIM, 0.0).astype(BF16)
    r = jnp.arange(TM)
    tri = (r[None, :] <= r[:, None]).astype(BF16)
    rm = jnp.arange(MOE_TM)
    before = (rm[:, None] < rm[None, :]).astype(BF16)

    states = []
    for i in range(DEPTH):
        w_pad = jnp.pad(w_in[i], ((0, 0), (0, IN_PAD - IN_WIDTH))).astype(BF16)
        bfp = jnp.pad(b_forget[i], (0, LANES - N_HEADS)).reshape(1, LANES)
        gq = (jnp.tile(g_q[i], N_HEADS) * ATT_SCALE).reshape(1, ATT_WIDTH)
        gk = jnp.tile(g_k[i], N_HEADS).reshape(1, ATT_WIDTH)
        q4, augq, k4, augk, vt, k, v, logf, u = _inproj(
            h, g_mix[i].reshape(1, D_MODEL), w_pad, bfp, gq, gk, bd, tri)
        gatt = g_out_att[i].reshape(1, ATT_WIDTH)

        attp = _attn_prompt(q4, augq, k4, augk.reshape(NT, TM, LANES), vt, gatt)

        atts = _sample_mixing(i, page_table, k_t, v_t, lf_t, q4[NT_PROMPT],
                              k[N_PROMPT:], v[N_PROMPT:], logf[N_PROMPT:], gatt)

        state_pad = jnp.pad(state_conv[i], ((0, 0), (HALO - (CONV_WIDTH - 1), 0), (0, 0)))
        w_dw_pad = jnp.pad(w_dw[i], ((0, HALO - CONV_WIDTH), (0, 0)))
        cn = _conv(u, state_pad, w_dw_pad, b_dw[i].reshape(1, CONV_CH), g_conv_ln[i].reshape(1, CONV_CH),
                   b_conv_ln[i].reshape(1, CONV_CH), g_out_conv[i].reshape(1, CONV_CH))

        wo = w_out[i].astype(BF16)
        p_all = jnp.concatenate([p_prompt[i].reshape(N_PROMPT, PLE_DIM), p_sample[i].reshape(N_SAMPLE, PLE_DIM)], axis=0)
        gple = g_ple[i].reshape(1, D_MODEL)
        wproj = w_ple_proj[i].astype(BF16)
        wgate = w_ple_gate[i].astype(BF16)
        gffn = g_ffn[i].reshape(1, D_MODEL)
        j = i // 2
        if i % 2 == 0:
            h1, n = _outproj(h, attp, atts, cn, wo[:ATT_WIDTH], wo[ATT_WIDTH:], gffn)
            h = _ffn_ple(h1, n, w_ff_gate[j].astype(BF16), w_ff_up[j].astype(BF16),
                         w_ff_down[j].astype(BF16), p_all, gple, wproj, wgate)
        else:
            wr = jnp.pad(w_router[j], ((0, 0), (0, LANES - N_EXPERTS))).astype(BF16)
            h1, n, gates, chosen, chosen_t = _outproj(h, attp, atts, cn, wo[:ATT_WIDTH], wo[ATT_WIDTH:], gffn,
                                                      router=wr)
            counts = jnp.sum(chosen[:, :N_EXPERTS].reshape(N_TOK // MOE_TM, MOE_TM, N_EXPERTS),
                             axis=1).astype(jnp.int32)
            h2 = _moe(counts, h1, n, gates, chosen_t, before, w_ex_gate[j].astype(BF16),
                      w_ex_up[j].astype(BF16), w_ex_down[j].astype(BF16))
            h = _ple_call(h2, p_all, gple, wproj, wgate)

        conv_p = u[:N_PROMPT].reshape(BATCH, SEQ, CONV_CH)[:, SEQ - (CONV_WIDTH - 1):]
        conv_s = jnp.concatenate([state_conv[i], u[N_PROMPT:].reshape(DEC_BATCH, DEC_SEQ, CONV_CH)],
                                 axis=1)[:, DEC_SEQ:]
        states.append((k, v, logf, conv_p, conv_s))

    def stack(idx, lo, hi, shape):
        return jnp.stack([s[idx][lo:hi].reshape(shape) for s in states])

    kv_p = (BATCH, SEQ, N_HEADS, HEAD_DIM)
    kv_s = (DEC_BATCH, DEC_SEQ, N_HEADS, HEAD_DIM)
    return (h[:N_PROMPT].reshape(BATCH, SEQ, D_MODEL),
            h[N_PROMPT:].reshape(DEC_BATCH, DEC_SEQ, D_MODEL),
            stack(0, 0, N_PROMPT, kv_p), stack(1, 0, N_PROMPT, kv_p),
            stack(2, 0, N_PROMPT, (BATCH, SEQ, N_HEADS)),
            jnp.stack([s[3] for s in states]),
            stack(0, N_PROMPT, N_TOK, kv_s), stack(1, N_PROMPT, N_TOK, kv_s),
            stack(2, N_PROMPT, N_TOK, (DEC_BATCH, DEC_SEQ, N_HEADS)),
            jnp.stack([s[4] for s in states]))
```

```python
import functools

import jax
import jax.numpy as jnp
from jax import lax
from jax.experimental import pallas as pl
from jax.experimental.pallas import tpu as pltpu

F32 = jnp.float32
BF16 = jnp.bfloat16

D_MODEL = 1024
BATCH = 2
SEQ = 8192
DEPTH = 4
DEC_BATCH = 32
DEC_SEQ = 8
PAST_LEN = 8192
PAGE_SIZE = 128
N_PAGES = PAST_LEN // PAGE_SIZE
N_HEADS = 8
HEAD_DIM = 64
ATT_WIDTH = N_HEADS * HEAD_DIM
CONV_CH = D_MODEL // 2
CONV_WIDTH = 31
IN_WIDTH = 3 * ATT_WIDTH + 2 * CONV_CH + N_HEADS
D_FF = 2816
N_EXPERTS = 8
D_FF_EXPERT = D_MODEL * 7 // 2
PLE_DIM = 256
EPS = 1e-6
ATT_SCALE = HEAD_DIM ** -0.5

LANES = 128
N_PROMPT = BATCH * SEQ
N_SAMPLE = DEC_BATCH * DEC_SEQ
N_TOK = N_PROMPT + N_SAMPLE
TM = 256
NT = N_TOK // TM
NT_PROMPT = N_PROMPT // TM
TILES_PER_SEQ = SEQ // TM
IN_PAD = 3 * ATT_WIDTH + 2 * CONV_CH + LANES
VT_ROWS = HEAD_DIM + 16
HALO = 32
PAGES_PER_STEP = 8
KEYS_PER_STEP = PAGES_PER_STEP * PAGE_SIZE
STEPS_PER_SEQ = N_PAGES // PAGES_PER_STEP
MOE_TM = 1280
MOE_TF = 896
MOE_CH = 128
NEG = -1e30
VMEM_LIMIT = 56 * 1024 * 1024


def _params(sem, vmem=VMEM_LIMIT):
    return pltpu.CompilerParams(dimension_semantics=sem, vmem_limit_bytes=vmem)


def _const_spec(shape):
    nd = len(shape)
    return pl.BlockSpec(shape, lambda *_: (0,) * nd, pipeline_mode=pl.Buffered(1))


def _split3(x):
    hi = x.astype(BF16)
    r = x - hi.astype(F32)
    mid = r.astype(BF16)
    lo = (r - mid.astype(F32)).astype(BF16)
    return hi, mid, lo


def _rms(x, g):
    return x * lax.rsqrt(jnp.mean(x * x, axis=-1, keepdims=True) + EPS) * g


def _sigmoid(x):
    return 1.0 / (1.0 + jnp.exp(-x))


def _silu(x):
    return x * _sigmoid(x)


def _log_sigmoid(x):
    return jnp.minimum(x, 0.0) - jnp.log1p(jnp.exp(-jnp.abs(x)))


def _inproj_kernel(h_ref, gmix_ref, w_ref, bf_ref, gq_ref, gk_ref, bd_ref, tri_ref,
                   q4_ref, augq_ref, k4_ref, augk_ref, vt_ref, k_ref, v_ref, logf_ref, u_ref,
                   carry_ref):
    i = pl.program_id(0)
    xn = _rms(h_ref[...], gmix_ref[...]).astype(BF16)

    def proj(c0, c1):
        return jnp.dot(xn, w_ref[:, c0:c1], preferred_element_type=F32)

    def head_norm(z, g):
        ms = jnp.dot((z * z).astype(BF16), bd_ref[...], preferred_element_type=F32)
        return z * lax.rsqrt(ms + EPS) * g

    q = head_norm(proj(0, 512), gq_ref[...])
    k = head_norm(proj(512, 1024), gk_ref[...])
    v = proj(1024, 1536)
    ua = proj(1536, 2048)
    ug = proj(2048, 2560)
    zf = proj(2560, IN_PAD)

    lane = lax.broadcasted_iota(jnp.int32, (TM, LANES), 1)
    logf = jnp.where(lane < N_HEADS, _log_sigmoid(zf + bf_ref[...]), 0.0)

    k_ref[...] = k
    v_ref[...] = v
    u_ref[...] = ua * _sigmoid(ug)
    logf_ref[...] = logf[:, :N_HEADS]

    qb = q.astype(BF16)
    kb = k.astype(BF16)
    for j in range(4):
        q4_ref[0, j] = qb[:, j * LANES:(j + 1) * LANES]
        k4_ref[0, j] = kb[:, j * LANES:(j + 1) * LANES]
    vt_ref[0, :, 0:HEAD_DIM, :] = v.T.astype(BF16).reshape(N_HEADS, HEAD_DIM, TM)
    vt_ref[0, :, HEAD_DIM:VT_ROWS, :] = jnp.ones((N_HEADS, VT_ROWS - HEAD_DIM, TM), BF16)

    hi, mid, lo = _split3(logf)
    tri = tri_ref[...]
    c = (jnp.dot(tri, hi, preferred_element_type=F32)
         + jnp.dot(tri, mid, preferred_element_type=F32)
         + jnp.dot(tri, lo, preferred_element_type=F32))
    carry = jnp.where(i % TILES_PER_SEQ == 0, 0.0, carry_ref[0:1, :])
    c = c + carry
    carry_ref[...] = jnp.broadcast_to(c[TM - 1:TM, :], (8, LANES))

    chi, cmid, clo = (p.astype(F32) for p in _split3(c))
    ones_q = jnp.where((lane >= 24) & (lane < 48), 1.0, 0.0)
    ones_k = jnp.where(lane < 24, 1.0, 0.0)
    augq = chi + pltpu.roll(cmid, 8, 1) + pltpu.roll(clo, 16, 1) + ones_q
    augk = ones_k - pltpu.roll(chi, 24, 1) - pltpu.roll(cmid, 32, 1) - pltpu.roll(clo, 40, 1)
    augq_ref[...] = augq.astype(BF16)
    augk_ref[...] = augk.astype(BF16)


def _inproj(h, gmix, w, bfp, gq, gk, bd, tri):
    row = lambda i: (i, 0)
    out_shape = (
        jax.ShapeDtypeStruct((NT, 4, TM, LANES), BF16),
        jax.ShapeDtypeStruct((N_TOK, LANES), BF16),
        jax.ShapeDtypeStruct((NT, 4, TM, LANES), BF16),
        jax.ShapeDtypeStruct((N_TOK, LANES), BF16),
        jax.ShapeDtypeStruct((NT, N_HEADS, VT_ROWS, TM), BF16),
        jax.ShapeDtypeStruct((N_TOK, ATT_WIDTH), F32),
        jax.ShapeDtypeStruct((N_TOK, ATT_WIDTH), F32),
        jax.ShapeDtypeStruct((N_TOK, N_HEADS), F32),
        jax.ShapeDtypeStruct((N_TOK, CONV_CH), F32),
    )
    out_specs = (
        pl.BlockSpec((1, 4, TM, LANES), lambda i: (i, 0, 0, 0)),
        pl.BlockSpec((TM, LANES), row),
        pl.BlockSpec((1, 4, TM, LANES), lambda i: (i, 0, 0, 0)),
        pl.BlockSpec((TM, LANES), row),
        pl.BlockSpec((1, N_HEADS, VT_ROWS, TM), lambda i: (i, 0, 0, 0)),
        pl.BlockSpec((TM, ATT_WIDTH), row),
        pl.BlockSpec((TM, ATT_WIDTH), row),
        pl.BlockSpec((TM, N_HEADS), row),
        pl.BlockSpec((TM, CONV_CH), row),
    )
    return pl.pallas_call(
        _inproj_kernel,
        grid=(NT,),
        in_specs=[pl.BlockSpec((TM, D_MODEL), row),
                  _const_spec((1, D_MODEL)), _const_spec((D_MODEL, IN_PAD)), _const_spec((1, LANES)),
                  _const_spec((1, ATT_WIDTH)), _const_spec((1, ATT_WIDTH)),
                  _const_spec((ATT_WIDTH, ATT_WIDTH)), _const_spec((TM, TM))],
        out_specs=out_specs,
        out_shape=out_shape,
        scratch_shapes=[pltpu.VMEM((8, LANES), F32)],
        compiler_params=_params(("arbitrary",)),
        name="inproj",
    )(h, gmix, w, bfp, gq, gk, bd, tri)


def _attn_kernel(q4_ref, augq_ref, k4_ref, augk_ref, vt_ref, gatt_ref, o_ref, qa_scr, m_scr, acc_scr, st_scr):
    qi = pl.program_id(1)
    lane = lax.broadcasted_iota(jnp.int32, (TM, LANES), 1)
    augq = augq_ref[...]
    zero = jnp.zeros((), BF16)
    for h in range(N_HEADS):
        qm = jnp.where((lane // HEAD_DIM) == (h % 2), q4_ref[0, h // 2], zero)
        am = jnp.where(((lane % 8) == h) & (lane < 48), augq, zero)
        qa_scr[h] = jnp.concatenate([qm, am], axis=1)
    m_scr[...] = jnp.full((N_HEADS, 1, TM), NEG, F32)
    acc_scr[...] = jnp.zeros((N_HEADS, VT_ROWS, TM), F32)

    key_pos = lax.broadcasted_iota(jnp.int32, (TM, TM), 0)
    qry_pos = lax.broadcasted_iota(jnp.int32, (TM, TM), 1)
    causal = key_pos <= qry_pos

    def scores(kb, slot, diagonal):
        aug = augk_ref[kb]
        for h in range(N_HEADS):
            ka = jnp.concatenate([k4_ref[kb, h // 2], aug], axis=1)
            st = lax.dot_general(ka, qa_scr[h], (((1,), (1,)), ((), ())),
                                 preferred_element_type=F32)
            st_scr[slot, h] = jnp.where(causal, st, NEG) if diagonal else st

    def softmax_pv(kb, slot):
        for h in range(N_HEADS):
            m = m_scr[h]
            m_new = jnp.maximum(m, jnp.max(st_scr[slot, h], axis=0, keepdims=True))
            p = jnp.exp(st_scr[slot, h] - m_new).astype(BF16)
            pv = jnp.dot(vt_ref[kb, h], p, preferred_element_type=F32)
            acc_scr[h] = jnp.exp(m - m_new) * acc_scr[h] + pv
            m_scr[h] = m_new

    @pl.when(qi == 0)
    def _():
        scores(0, 0, True)
        softmax_pv(0, 0)

    @pl.when(qi > 0)
    def _():
        scores(0, 0, False)

    def body(kk, carry):
        kb = 2 * kk
        scores(kb + 1, 1, False)
        softmax_pv(kb, 0)
        scores(kb + 2, 0, False)
        softmax_pv(kb + 1, 1)
        return carry

    lax.fori_loop(0, jnp.maximum(qi - 1, 0) // 2, body, 0)

    @pl.when(qi % 2 == 1)
    def _():
        scores(qi, 1, True)
        softmax_pv(qi - 1, 0)
        softmax_pv(qi, 1)

    @pl.when((qi % 2 == 0) & (qi > 0))
    def _():
        scores(qi - 1, 1, False)
        softmax_pv(qi - 2, 0)
        scores(qi, 0, True)
        softmax_pv(qi - 1, 1)
        softmax_pv(qi, 0)

    acc = acc_scr[...]
    ot = acc[:, 0:HEAD_DIM, :] / acc[:, HEAD_DIM:HEAD_DIM + 1, :]
    o = ot.reshape(ATT_WIDTH, TM).T
    o_ref[...] = _rms(o, gatt_ref[...]).astype(BF16)


def _attn_prompt(q4, augq, k4, augk3, vt, gatt):
    seq_blk = lambda b, qi: (b, 0, 0, 0)
    once = pl.Buffered(1)
    return pl.pallas_call(
        _attn_kernel,
        grid=(BATCH, TILES_PER_SEQ),
        in_specs=[pl.BlockSpec((1, 4, TM, LANES), lambda b, qi: (b * TILES_PER_SEQ + qi, 0, 0, 0)),
                  pl.BlockSpec((TM, LANES), lambda b, qi: (b * TILES_PER_SEQ + qi, 0)),
                  pl.BlockSpec((TILES_PER_SEQ, 4, TM, LANES), seq_blk, pipeline_mode=once),
                  pl.BlockSpec((TILES_PER_SEQ, TM, LANES), lambda b, qi: (b, 0, 0), pipeline_mode=once),
                  pl.BlockSpec((TILES_PER_SEQ, N_HEADS, VT_ROWS, TM), seq_blk, pipeline_mode=once),
                  _const_spec((1, ATT_WIDTH))],
        out_specs=pl.BlockSpec((TM, ATT_WIDTH), lambda b, qi: (b * TILES_PER_SEQ + qi, 0)),
        out_shape=jax.ShapeDtypeStruct((N_PROMPT, ATT_WIDTH), BF16),
        scratch_shapes=[pltpu.VMEM((N_HEADS, TM, 2 * LANES), BF16),
                        pltpu.VMEM((N_HEADS, 1, TM), F32),
                        pltpu.VMEM((N_HEADS, VT_ROWS, TM), F32),
                        pltpu.VMEM((2, N_HEADS, TM, TM), F32)],
        compiler_params=_params(("arbitrary", "arbitrary")),
        name="attn_prompt",
    )(q4, augq, k4, augk3, vt, gatt)


def _attn_sample_kernel(pt_ref, *refs):
    del pt_ref
    n = PAGES_PER_STEP
    kp, vp, fp = refs[:n], refs[n:2 * n], refs[2 * n:3 * n]
    (qblk_ref, lfn_ref, knew_ref, vnew_ref, upper_ref, tri_ref, gatt_ref,
     o_ref, m_scr, l_scr, acc_scr, cb_scr, carry_scr) = refs[3 * n:]
    s = pl.program_id(1)
    rows = DEC_SEQ * N_HEADS
    qblk = qblk_ref[0]
    row = lax.broadcasted_iota(jnp.int32, (rows, LANES), 0)
    lane = lax.broadcasted_iota(jnp.int32, (rows, LANES), 1)
    contract_last = (((1,), (1,)), ((), ()))

    def over_tokens(x, mat_ref):
        return sum(jnp.dot(p, mat_ref[...], preferred_element_type=F32) for p in _split3(x))

    @pl.when(s == 0)
    def _():
        cn_t = over_tokens(lfn_ref[0], tri_ref)
        cn_rows = jnp.concatenate([cn_t] * DEC_SEQ, axis=0)
        cb = jnp.sum(jnp.where(lane == row // N_HEADS, cn_rows, 0.0), axis=-1, keepdims=True)
        cb_scr[...] = jnp.broadcast_to(cb, (rows, LANES))
        carry_scr[...] = jnp.zeros((N_HEADS, LANES), F32)
        pad = jnp.zeros((LANES - DEC_SEQ, ATT_WIDTH), BF16)
        kn = jnp.concatenate([knew_ref[0].astype(BF16), pad], axis=0)
        vn = jnp.concatenate([vnew_ref[0].astype(BF16), pad], axis=0)
        st = lax.dot_general(qblk, kn, contract_last, preferred_element_type=F32)
        st = jnp.where((lane < DEC_SEQ) & (lane <= row // N_HEADS), st + cb - cn_rows, NEG)
        m = jnp.max(st, axis=-1, keepdims=True)
        p = jnp.exp(st - m)
        m_scr[...] = jnp.broadcast_to(m, (rows, LANES))
        l_scr[...] = jnp.broadcast_to(jnp.sum(p, axis=-1, keepdims=True), (rows, LANES))
        acc_scr[...] = jnp.dot(p.astype(BF16), vn, preferred_element_type=F32)

    lf = jnp.concatenate([r[0, 0] for r in fp], axis=0)
    inside = over_tokens(lf, upper_ref)
    total = jnp.sum(lf, axis=-1, keepdims=True)
    carry = carry_scr[...]
    bias = []
    for j in range(n):
        d = inside[j * N_HEADS:(j + 1) * N_HEADS] + carry
        carry = carry + total[j * N_HEADS:(j + 1) * N_HEADS]
        bias.append(jnp.concatenate([d] * DEC_SEQ, axis=0))
    carry_scr[...] = carry
    cb = cb_scr[...]
    bias = jnp.concatenate([b + cb for b in bias], axis=1)

    kt = jnp.concatenate([r[0, 0].reshape(ATT_WIDTH, PAGE_SIZE).astype(BF16) for r in kp], axis=1)
    st = jnp.dot(qblk, kt, preferred_element_type=F32) + bias
    m_old = m_scr[:, 0:1]
    m_new = jnp.maximum(m_old, jnp.max(st, axis=-1, keepdims=True))
    alpha = jnp.exp(m_old - m_new)
    p = jnp.exp(st - m_new)
    l_new = alpha * l_scr[:, 0:1] + jnp.sum(p, axis=-1, keepdims=True)
    vt = jnp.concatenate([r[0, 0].reshape(ATT_WIDTH, PAGE_SIZE).astype(BF16) for r in vp], axis=1)
    pv = lax.dot_general(p.astype(BF16), vt, contract_last, preferred_element_type=F32)
    acc = alpha * acc_scr[...] + pv
    m_scr[...] = jnp.broadcast_to(m_new, (rows, LANES))
    l_scr[...] = jnp.broadcast_to(l_new, (rows, LANES))
    acc_scr[...] = acc

    @pl.when(s == STEPS_PER_SEQ - 1)
    def _():
        o = acc / l_new
        r2 = lax.broadcasted_iota(jnp.int32, (rows, ATT_WIDTH), 0)
        c2 = lax.broadcasted_iota(jnp.int32, (rows, ATT_WIDTH), 1)
        o = jnp.where(c2 // HEAD_DIM == r2 % N_HEADS, o, 0.0)
        att = jnp.sum(o.reshape(DEC_SEQ, N_HEADS, ATT_WIDTH), axis=1)
        o_ref[...] = _rms(att, gatt_ref[...])


def _attn_sample(layer, page_table, k_t, v_t, lf_t, qblk, lfn, knew, vnew, upper, tri, gatt):
    def page(j):
        return lambda b, s, pt: pt[b, N_PAGES - 1 - (s * PAGES_PER_STEP + j)]

    def kv_spec(j):
        return pl.BlockSpec((1, 1, N_HEADS, HEAD_DIM, PAGE_SIZE),
                            lambda b, s, pt: (layer, page(j)(b, s, pt), 0, 0, 0))

    def lf_spec(j):
        return pl.BlockSpec((1, 1, N_HEADS, PAGE_SIZE), lambda b, s, pt: (layer, page(j)(b, s, pt), 0, 0))

    rows = DEC_SEQ * N_HEADS
    per_seq = lambda shape: pl.BlockSpec((1,) + shape, lambda b, s, pt: (b, 0, 0))
    const = lambda shape: pl.BlockSpec(shape, lambda b, s, pt: (0, 0))
    pages = range(PAGES_PER_STEP)
    grid_spec = pltpu.PrefetchScalarGridSpec(
        num_scalar_prefetch=1,
        grid=(DEC_BATCH, STEPS_PER_SEQ),
        in_specs=([kv_spec(j) for j in pages] + [kv_spec(j) for j in pages] + [lf_spec(j) for j in pages]
                  + [per_seq((rows, ATT_WIDTH)), per_seq((N_HEADS, LANES)),
                     per_seq((DEC_SEQ, ATT_WIDTH)), per_seq((DEC_SEQ, ATT_WIDTH)),
                     const((PAGE_SIZE, PAGE_SIZE)), const((LANES, LANES)), const((1, ATT_WIDTH))]),
        out_specs=pl.BlockSpec((DEC_SEQ, ATT_WIDTH), lambda b, s, pt: (b, 0)),
        scratch_shapes=[pltpu.VMEM((rows, LANES), F32), pltpu.VMEM((rows, LANES), F32),
                        pltpu.VMEM((rows, ATT_WIDTH), F32), pltpu.VMEM((rows, LANES), F32),
                        pltpu.VMEM((N_HEADS, LANES), F32)],
    )
    return pl.pallas_call(
        _attn_sample_kernel,
        grid_spec=grid_spec,
        out_shape=jax.ShapeDtypeStruct((N_SAMPLE, ATT_WIDTH), F32),
        compiler_params=_params(("arbitrary", "arbitrary")),
        name="attn_sample",
    )(page_table, *([k_t] * PAGES_PER_STEP), *([v_t] * PAGES_PER_STEP), *([lf_t] * PAGES_PER_STEP),
      qblk, lfn, knew, vnew, upper, tri, gatt)


def _conv_kernel(u_ref, prev_ref, state_ref, w_ref, bdw_ref, gln_ref, bln_ref, gcv_ref,
                 o_ref, xs_scr, xs3_scr, y_scr):
    t = pl.program_id(0)

    @pl.when(t < NT_PROMPT)
    def _():
        first = (t % TILES_PER_SEQ) == 0
        xs_scr[0:HALO, :] = jnp.where(first, 0.0, prev_ref[...])
        xs_scr[HALO:, :] = u_ref[...]
        acc = jnp.zeros((TM, CONV_CH), F32)
        for w in range(CONV_WIDTH):
            off = w + HALO - (CONV_WIDTH - 1)
            acc = acc + xs_scr[off:off + TM, :] * w_ref[w:w + 1, :]
        y_scr[...] = acc

    @pl.when(t == NT_PROMPT)
    def _():
        xs3_scr[:, 0:HALO, :] = state_ref[...]
        xs3_scr[:, HALO:, :] = u_ref[...].reshape(DEC_BATCH, DEC_SEQ, CONV_CH)
        acc = jnp.zeros((DEC_BATCH, DEC_SEQ, CONV_CH), F32)
        for w in range(CONV_WIDTH):
            off = w + HALO - (CONV_WIDTH - 1)
            acc = acc + xs3_scr[:, off:off + DEC_SEQ, :] * w_ref[w:w + 1, :]
        y_scr[...] = acc.reshape(TM, CONV_CH)

    y = y_scr[...] + bdw_ref[...]
    yc = y - jnp.mean(y, axis=-1, keepdims=True)
    yn = yc * lax.rsqrt(jnp.mean(yc * yc, axis=-1, keepdims=True) + EPS) * gln_ref[...] + bln_ref[...]
    o_ref[...] = _rms(_silu(yn), gcv_ref[...]).astype(BF16)


def _conv(u, state_pad, w, bdw, gln, bln, gcv):
    halo_blocks = TM // HALO
    return pl.pallas_call(
        _conv_kernel,
        grid=(NT,),
        in_specs=[pl.BlockSpec((TM, CONV_CH), lambda t: (t, 0)),
                  pl.BlockSpec((HALO, CONV_CH), lambda t: (jnp.maximum(t * halo_blocks - 1, 0), 0)),
                  _const_spec((DEC_BATCH, HALO, CONV_CH)),
                  _const_spec((HALO, CONV_CH)),
                  _const_spec((1, CONV_CH)), _const_spec((1, CONV_CH)),
                  _const_spec((1, CONV_CH)), _const_spec((1, CONV_CH))],
        out_specs=pl.BlockSpec((TM, CONV_CH), lambda t: (t, 0)),
        out_shape=jax.ShapeDtypeStruct((N_TOK, CONV_CH), BF16),
        scratch_shapes=[pltpu.VMEM((TM + HALO, CONV_CH), F32),
                        pltpu.VMEM((DEC_BATCH, HALO + DEC_SEQ, CONV_CH), F32),
                        pltpu.VMEM((TM, CONV_CH), F32)],
        compiler_params=_params(("arbitrary",)),
        name="conv",
    )(u, u, state_pad, w, bdw, gln, bln, gcv)


def _top2_gates(logits):
    lane = lax.broadcasted_iota(jnp.int32, logits.shape, 1)
    m1 = jnp.max(logits, axis=-1, keepdims=True)
    i1 = jnp.min(jnp.where(logits == m1, lane, LANES), axis=-1, keepdims=True)
    rest = jnp.where(lane == i1, NEG, logits)
    m2 = jnp.max(rest, axis=-1, keepdims=True)
    i2 = jnp.min(jnp.where(rest == m2, lane, LANES), axis=-1, keepdims=True)
    e = jnp.exp(m2 - m1)
    w1 = 1.0 / (1.0 + e)
    w2 = e / (1.0 + e)
    gates = jnp.where(lane == i1, w1, 0.0) + jnp.where(lane == i2, w2, 0.0)
    chosen = jnp.where((lane == i1) | (lane == i2), 1.0, 0.0)
    return gates, chosen


def _outproj_kernel(*refs, with_router):
    if with_router:
        (h_ref, attp_ref, atts_ref, cn_ref, wa_ref, wc_ref, gffn_ref, wr_ref,
         h1_ref, n_ref, gates_ref, chosen_ref, chosen_t_ref) = refs
    else:
        h_ref, attp_ref, atts_ref, cn_ref, wa_ref, wc_ref, gffn_ref, h1_ref, n_ref = refs
    t = pl.program_id(0)
    att = jnp.where(t == NT_PROMPT, atts_ref[...].astype(BF16), attp_ref[...])
    h1 = (h_ref[...]
          + jnp.dot(att, wa_ref[...], preferred_element_type=F32)
          + jnp.dot(cn_ref[...], wc_ref[...], preferred_element_type=F32))
    h1_ref[...] = h1
    n = _rms(h1, gffn_ref[...])
    n_ref[...] = n.astype(BF16)
    if with_router:
        logits = jnp.dot(n.astype(BF16), wr_ref[...], preferred_element_type=F32)
        lane = lax.broadcasted_iota(jnp.int32, logits.shape, 1)
        gates, chosen = _top2_gates(jnp.where(lane < N_EXPERTS, logits, NEG))
        gates_ref[...] = gates
        chosen_ref[...] = chosen
        chosen_t_ref[...] = chosen.T[:N_EXPERTS]


def _outproj(h, attp, atts, cn, wa, wc, gffn, router=None):
    row = lambda t: (t, 0)
    in_specs = [pl.BlockSpec((TM, D_MODEL), row),
                pl.BlockSpec((TM, ATT_WIDTH), lambda t: (jnp.minimum(t, NT_PROMPT - 1), 0)),
                _const_spec((N_SAMPLE, ATT_WIDTH)),
                pl.BlockSpec((TM, CONV_CH), row),
                _const_spec((ATT_WIDTH, D_MODEL)), _const_spec((CONV_CH, D_MODEL)),
                _const_spec((1, D_MODEL))]
    out_specs = [pl.BlockSpec((TM, D_MODEL), row), pl.BlockSpec((TM, D_MODEL), row)]
    out_shape = [jax.ShapeDtypeStruct((N_TOK, D_MODEL), F32), jax.ShapeDtypeStruct((N_TOK, D_MODEL), BF16)]
    args = [h, attp, atts, cn, wa, wc, gffn]
    if router is not None:
        in_specs += [_const_spec((D_MODEL, LANES))]
        out_specs += [pl.BlockSpec((TM, LANES), row), pl.BlockSpec((TM, LANES), row),
                      pl.BlockSpec((N_EXPERTS, TM), lambda t: (0, t))]
        out_shape += [jax.ShapeDtypeStruct((N_TOK, LANES), F32), jax.ShapeDtypeStruct((N_TOK, LANES), F32),
                      jax.ShapeDtypeStruct((N_EXPERTS, N_TOK), F32)]
        args.append(router)
    return pl.pallas_call(
        functools.partial(_outproj_kernel, with_router=router is not None),
        grid=(NT,),
        in_specs=in_specs,
        out_specs=tuple(out_specs),
        out_shape=tuple(out_shape),
        compiler_params=_params(("arbitrary",)),
        name="outproj",
    )(*args)


def _ple(h2, pp_ref, ps_ref, gple_ref, wproj_ref, wgate_ref):
    p = jnp.where(pl.program_id(0) == NT_PROMPT, ps_ref[0], pp_ref[...])
    gate = _sigmoid(jnp.dot(_rms(h2, gple_ref[...]).astype(BF16), wgate_ref[...],
                            preferred_element_type=F32))
    proj = jnp.dot(p.astype(BF16), wproj_ref[...], preferred_element_type=F32)
    return h2 + proj * gate


def _ple_specs(layer):
    return [pl.BlockSpec((TM, PLE_DIM), lambda t: (layer * NT_PROMPT + jnp.minimum(t, NT_PROMPT - 1), 0)),
            pl.BlockSpec((1, N_SAMPLE, PLE_DIM), lambda t: (layer, 0, 0))]


def _ffn_ple_kernel(h1_ref, n_ref, wg_ref, wu_ref, wd_ref, pp_ref, ps_ref, gple_ref, wproj_ref, wgate_ref, o_ref):
    n = n_ref[...]
    g = jnp.dot(n, wg_ref[...], preferred_element_type=F32)
    u = jnp.dot(n, wu_ref[...], preferred_element_type=F32)
    a = (_silu(g) * u).astype(BF16)
    h2 = h1_ref[...] + jnp.dot(a, wd_ref[...], preferred_element_type=F32)
    o_ref[...] = _ple(h2, pp_ref, ps_ref, gple_ref, wproj_ref, wgate_ref)


def _ffn_ple(layer, h1, n, wg, wu, wd, pp, ps, gple, wproj, wgate):
    row = lambda t: (t, 0)
    return pl.pallas_call(
        _ffn_ple_kernel,
        grid=(NT,),
        in_specs=[pl.BlockSpec((TM, D_MODEL), row), pl.BlockSpec((TM, D_MODEL), row),
                  _const_spec((D_MODEL, D_FF)), _const_spec((D_MODEL, D_FF)), _const_spec((D_FF, D_MODEL)),
                  *_ple_specs(layer), _const_spec((1, D_MODEL)),
                  _const_spec((PLE_DIM, D_MODEL)), _const_spec((D_MODEL, D_MODEL))],
        out_specs=pl.BlockSpec((TM, D_MODEL), row),
        out_shape=jax.ShapeDtypeStruct((N_TOK, D_MODEL), F32),
        compiler_params=_params(("arbitrary",)),
        name="ffn_ple",
    )(h1, n, wg, wu, wd, pp, ps, gple, wproj, wgate)


def _moe_kernel(cnt_ref, h1_ref, n_ref, gates_ref, chosen_t_ref, before_ref, wg_ref, wu_ref, wd_ref,
                o_ref, acc_scr, slot_scr, xe_scr, ye_scr):
    t = pl.program_id(0)
    e = pl.program_id(1)
    c = pl.program_id(2)
    last_c = pl.num_programs(2) - 1
    n_chunks = (cnt_ref[t, e] + MOE_CH - 1) // MOE_CH

    @pl.when((e == 0) & (c == 0))
    def _():
        acc_scr[...] = h1_ref[...]
        chosen_t = chosen_t_ref[...]
        before = jnp.dot(chosen_t.astype(BF16), before_ref[...], preferred_element_type=F32)
        slot_scr[...] = jnp.where(chosen_t > 0.5, before, -1.0)

    def selection(j):
        slot = slot_scr[pl.ds(e, 1), :]
        row = (lax.broadcasted_iota(jnp.int32, (MOE_CH, MOE_TM), 0) + j * MOE_CH).astype(F32)
        return jnp.where(slot == row, 1.0, 0.0).astype(BF16)

    @pl.when(c == 0)
    def _():
        def gather(j, carry):
            r0 = pl.multiple_of(j * MOE_CH, MOE_CH)
            xe_scr[pl.ds(r0, MOE_CH), :] = jnp.dot(
                selection(j), n_ref[...], preferred_element_type=F32).astype(BF16)
            return carry
        lax.fori_loop(0, n_chunks, gather, 0)

    def expert(j, carry):
        r0 = pl.multiple_of(j * MOE_CH, MOE_CH)
        x = xe_scr[pl.ds(r0, MOE_CH), :]
        g = jnp.dot(x, wg_ref[0, 0], preferred_element_type=F32)
        u = jnp.dot(x, wu_ref[0, 0], preferred_element_type=F32)
        y = jnp.dot((_silu(g) * u).astype(BF16), wd_ref[0, 0], preferred_element_type=F32)

        @pl.when(c == 0)
        def _():
            ye_scr[pl.ds(r0, MOE_CH), :] = y

        @pl.when(c > 0)
        def _():
            ye_scr[pl.ds(r0, MOE_CH), :] += y
        return carry

    lax.fori_loop(0, n_chunks, expert, 0)

    @pl.when(c == last_c)
    def _():
        lane = lax.broadcasted_iota(jnp.int32, (MOE_TM, LANES), 1)
        gate = jnp.sum(jnp.where(lane == e, gates_ref[...], 0.0), axis=-1, keepdims=True)

        def scatter(j, carry):
            r0 = pl.multiple_of(j * MOE_CH, MOE_CH)
            y = ye_scr[pl.ds(r0, MOE_CH), :]
            y_hi = y.astype(BF16)
            y_lo = (y - y_hi.astype(F32)).astype(BF16)
            sel = selection(j)
            back = lax.dot_general(jnp.concatenate([sel, sel], axis=0), jnp.concatenate([y_hi, y_lo], axis=0),
                                   (((0,), (0,)), ((), ())), preferred_element_type=F32)
            acc_scr[...] += gate * back
            return carry
        lax.fori_loop(0, n_chunks, scatter, 0)

    @pl.when((e == N_EXPERTS - 1) & (c == last_c))
    def _():
        o_ref[...] = acc_scr[...]


def _moe(layer, counts, h1, n, gates, chosen_t, before, wg, wu, wd):
    row = lambda t, e, c, cnt: (t, 0)
    once = pl.Buffered(1)
    grid_spec = pltpu.PrefetchScalarGridSpec(
        num_scalar_prefetch=1,
        grid=(N_TOK // MOE_TM, N_EXPERTS, D_FF_EXPERT // MOE_TF),
        in_specs=[pl.BlockSpec((MOE_TM, D_MODEL), row, pipeline_mode=once),
                  pl.BlockSpec((MOE_TM, D_MODEL), row, pipeline_mode=once),
                  pl.BlockSpec((MOE_TM, LANES), row, pipeline_mode=once),
                  pl.BlockSpec((N_EXPERTS, MOE_TM), lambda t, e, c, cnt: (0, t), pipeline_mode=once),
                  pl.BlockSpec((MOE_TM, MOE_TM), lambda t, e, c, cnt: (0, 0), pipeline_mode=once),
                  pl.BlockSpec((1, 1, D_MODEL, MOE_TF), lambda t, e, c, cnt: (layer, e, 0, c)),
                  pl.BlockSpec((1, 1, D_MODEL, MOE_TF), lambda t, e, c, cnt: (layer, e, 0, c)),
                  pl.BlockSpec((1, 1, MOE_TF, D_MODEL), lambda t, e, c, cnt: (layer, e, c, 0))],
        out_specs=pl.BlockSpec((MOE_TM, D_MODEL), row),
        scratch_shapes=[pltpu.VMEM((MOE_TM, D_MODEL), F32),
                        pltpu.VMEM((N_EXPERTS, MOE_TM), F32),
                        pltpu.VMEM((MOE_TM, D_MODEL), BF16),
                        pltpu.VMEM((MOE_TM, D_MODEL), F32)],
    )
    return pl.pallas_call(
        _moe_kernel,
        grid_spec=grid_spec,
        out_shape=jax.ShapeDtypeStruct((N_TOK, D_MODEL), F32),
        compiler_params=_params(("arbitrary", "arbitrary", "arbitrary")),
        name="moe",
    )(counts, h1, n, gates, chosen_t, before, wg, wu, wd)


def _ple_kernel(h2_ref, pp_ref, ps_ref, gple_ref, wproj_ref, wgate_ref, o_ref):
    o_ref[...] = _ple(h2_ref[...], pp_ref, ps_ref, gple_ref, wproj_ref, wgate_ref)


def _ple_call(layer, h2, pp, ps, gple, wproj, wgate):
    row = lambda t: (t, 0)
    return pl.pallas_call(
        _ple_kernel,
        grid=(NT,),
        in_specs=[pl.BlockSpec((TM, D_MODEL), row), *_ple_specs(layer),
                  _const_spec((1, D_MODEL)), _const_spec((PLE_DIM, D_MODEL)),
                  _const_spec((D_MODEL, D_MODEL))],
        out_specs=pl.BlockSpec((TM, D_MODEL), row),
        out_shape=jax.ShapeDtypeStruct((N_TOK, D_MODEL), F32),
        compiler_params=_params(("arbitrary",)),
        name="ple",
    )(h2, pp, ps, gple, wproj, wgate)


def _sample_mixing(layer, page_table, k_t, v_t, lf_t, q_tile, k_new, v_new, logf_new, gatt):
    tok = jnp.arange(PAGE_SIZE)
    upper = (tok[:, None] > tok[None, :]).astype(BF16)
    tri = (tok[:, None] <= tok[None, :]).astype(BF16)
    q_s = q_tile.transpose(1, 0, 2).reshape(DEC_BATCH, DEC_SEQ, N_HEADS, HEAD_DIM)
    eye = jnp.eye(N_HEADS, dtype=BF16)
    qblk = (q_s[:, :, :, None, :] * eye[None, None, :, :, None]).reshape(
        DEC_BATCH, DEC_SEQ * N_HEADS, ATT_WIDTH)
    lfn = jnp.pad(logf_new.reshape(DEC_BATCH, DEC_SEQ, N_HEADS).transpose(0, 2, 1),
                  ((0, 0), (0, 0), (0, LANES - DEC_SEQ)))
    return _attn_sample(layer, page_table, k_t, v_t, lf_t, qblk, lfn,
                        k_new.reshape(DEC_BATCH, DEC_SEQ, ATT_WIDTH), v_new.reshape(DEC_BATCH, DEC_SEQ, ATT_WIDTH),
                        upper, tri, gatt)


def kernel(x_prompt, x_sample, cache_k, cache_v, cache_logf, state_conv, page_table, p_prompt, p_sample, g_mix, w_in, b_forget, g_q, g_k, w_dw, b_dw, g_conv_ln, b_conv_ln, g_out_att, g_out_conv, w_out, g_ffn, w_ff_gate, w_ff_up, w_ff_down, w_router, w_ex_gate, w_ex_up, w_ex_down, g_ple, w_ple_proj, w_ple_gate):
    h = jnp.concatenate([x_prompt.reshape(N_PROMPT, D_MODEL), x_sample.reshape(N_SAMPLE, D_MODEL)], axis=0)
    k_t = jnp.transpose(cache_k, (0, 1, 3, 4, 2))
    v_t = jnp.transpose(cache_v, (0, 1, 3, 4, 2))
    lf_t = jnp.transpose(cache_logf, (0, 1, 3, 2))
    wex_gate, wex_up, wex_down = w_ex_gate.astype(BF16), w_ex_up.astype(BF16), w_ex_down.astype(BF16)
    pp = p_prompt.reshape(DEPTH * N_PROMPT, PLE_DIM)
    ps = p_sample.reshape(DEPTH, N_SAMPLE, PLE_DIM)

    idx = jnp.arange(ATT_WIDTH)
    bd = jnp.where(idx[:, None] // HEAD_DIM == idx[None, :] // HEAD_DIM, 1.0 / HEAD_DIM, 0.0).astype(BF16)
    r = jnp.arange(TM)
    tri = (r[None, :] <= r[:, None]).astype(BF16)
    rm = jnp.arange(MOE_TM)
    before = (rm[:, None] < rm[None, :]).astype(BF16)

    states = []
    for i in range(DEPTH):
        w_pad = jnp.pad(w_in[i], ((0, 0), (0, IN_PAD - IN_WIDTH))).astype(BF16)
        bfp = jnp.pad(b_forget[i], (0, LANES - N_HEADS)).reshape(1, LANES)
        gq = (jnp.tile(g_q[i], N_HEADS) * ATT_SCALE).reshape(1, ATT_WIDTH)
        gk = jnp.tile(g_k[i], N_HEADS).reshape(1, ATT_WIDTH)
        q4, augq, k4, augk, vt, k, v, logf, u = _inproj(
            h, g_mix[i].reshape(1, D_MODEL), w_pad, bfp, gq, gk, bd, tri)
        gatt = g_out_att[i].reshape(1, ATT_WIDTH)

        attp = _attn_prompt(q4, augq, k4, augk.reshape(NT, TM, LANES), vt, gatt)

        atts = _sample_mixing(i, page_table, k_t, v_t, lf_t, q4[NT_PROMPT],
                              k[N_PROMPT:], v[N_PROMPT:], logf[N_PROMPT:], gatt)

        state_pad = jnp.pad(state_conv[i], ((0, 0), (HALO - (CONV_WIDTH - 1), 0), (0, 0)))
        w_dw_pad = jnp.pad(w_dw[i], ((0, HALO - CONV_WIDTH), (0, 0)))
        cn = _conv(u, state_pad, w_dw_pad, b_dw[i].reshape(1, CONV_CH), g_conv_ln[i].reshape(1, CONV_CH),
                   b_conv_ln[i].reshape(1, CONV_CH), g_out_conv[i].reshape(1, CONV_CH))

        wo = w_out[i].astype(BF16)
        gple = g_ple[i].reshape(1, D_MODEL)
        wproj = w_ple_proj[i].astype(BF16)
        wgate = w_ple_gate[i].astype(BF16)
        gffn = g_ffn[i].reshape(1, D_MODEL)
        j = i // 2
        if i % 2 == 0:
            h1, n = _outproj(h, attp, atts, cn, wo[:ATT_WIDTH], wo[ATT_WIDTH:], gffn)
            h = _ffn_ple(i, h1, n, w_ff_gate[j].astype(BF16), w_ff_up[j].astype(BF16),
                         w_ff_down[j].astype(BF16), pp, ps, gple, wproj, wgate)
        else:
            wr = jnp.pad(w_router[j], ((0, 0), (0, LANES - N_EXPERTS))).astype(BF16)
            h1, n, gates, chosen, chosen_t = _outproj(h, attp, atts, cn, wo[:ATT_WIDTH], wo[ATT_WIDTH:], gffn,
                                                      router=wr)
            counts = jnp.sum(chosen[:, :N_EXPERTS].reshape(N_TOK // MOE_TM, MOE_TM, N_EXPERTS),
                             axis=1).astype(jnp.int32)
            h2 = _moe(j, counts, h1, n, gates, chosen_t, before, wex_gate, wex_up, wex_down)
            h = _ple_call(i, h2, pp, ps, gple, wproj, wgate)

        conv_p = u[:N_PROMPT].reshape(BATCH, SEQ, CONV_CH)[:, SEQ - (CONV_WIDTH - 1):]
        conv_s = jnp.concatenate([state_conv[i], u[N_PROMPT:].reshape(DEC_BATCH, DEC_SEQ, CONV_CH)],
                                 axis=1)[:, DEC_SEQ:]
        states.append((k, v, logf, conv_p, conv_s))

    def stack(idx, lo, hi, shape):
        return jnp.stack([s[idx][lo:hi].reshape(shape) for s in states])

    kv_p = (BATCH, SEQ, N_HEADS, HEAD_DIM)
    kv_s = (DEC_BATCH, DEC_SEQ, N_HEADS, HEAD_DIM)
    return (h[:N_PROMPT].reshape(BATCH, SEQ, D_MODEL),
            h[N_PROMPT:].reshape(DEC_BATCH, DEC_SEQ, D_MODEL),
            stack(0, 0, N_PROMPT, kv_p), stack(1, 0, N_PROMPT, kv_p),
            stack(2, 0, N_PROMPT, (BATCH, SEQ, N_HEADS)),
            jnp.stack([s[3] for s in states]),
            stack(0, N_PROMPT, N_TOK, kv_s), stack(1, N_PROMPT, N_TOK, kv_s),
            stack(2, N_PROMPT, N_TOK, (DEC_BATCH, DEC_SEQ, N_HEADS)),
            jnp.stack([s[4] for s in states]))
```

```python
import functools

import jax
import jax.numpy as jnp
from jax import lax
from jax.experimental import pallas as pl
from jax.experimental.pallas import tpu as pltpu

F32 = jnp.float32
BF16 = jnp.bfloat16

D_MODEL = 1024
BATCH = 2
SEQ = 8192
DEPTH = 4
DEC_BATCH = 32
DEC_SEQ = 8
PAST_LEN = 8192
PAGE_SIZE = 128
N_PAGES = PAST_LEN // PAGE_SIZE
N_HEADS = 8
HEAD_DIM = 64
ATT_WIDTH = N_HEADS * HEAD_DIM
CONV_CH = D_MODEL // 2
CONV_WIDTH = 31
IN_WIDTH = 3 * ATT_WIDTH + 2 * CONV_CH + N_HEADS
D_FF = 2816
N_EXPERTS = 8
D_FF_EXPERT = D_MODEL * 7 // 2
PLE_DIM = 256
EPS = 1e-6
ATT_SCALE = HEAD_DIM ** -0.5

LANES = 128
N_PROMPT = BATCH * SEQ
N_SAMPLE = DEC_BATCH * DEC_SEQ
N_TOK = N_PROMPT + N_SAMPLE
TM = 256
NT = N_TOK // TM
NT_PROMPT = N_PROMPT // TM
TILES_PER_SEQ = SEQ // TM
IN_PAD = 3 * ATT_WIDTH + 2 * CONV_CH + LANES
VT_ROWS = HEAD_DIM + 16
HALO = 32
PAGES_PER_STEP = 16
KEYS_PER_STEP = PAGES_PER_STEP * PAGE_SIZE
STEPS_PER_SEQ = N_PAGES // PAGES_PER_STEP
MOE_TM = 1280
MOE_TF = 896
MOE_CH = 128
NEG = -1e30
VMEM_LIMIT = 56 * 1024 * 1024


def _params(sem, vmem=VMEM_LIMIT):
    return pltpu.CompilerParams(dimension_semantics=sem, vmem_limit_bytes=vmem)


def _const_spec(shape):
    nd = len(shape)
    return pl.BlockSpec(shape, lambda *_: (0,) * nd, pipeline_mode=pl.Buffered(1))


def _split3(x):
    hi = x.astype(BF16)
    r = x - hi.astype(F32)
    mid = r.astype(BF16)
    lo = (r - mid.astype(F32)).astype(BF16)
    return hi, mid, lo


def _rms(x, g):
    return x * lax.rsqrt(jnp.mean(x * x, axis=-1, keepdims=True) + EPS) * g


def _sigmoid(x):
    return 1.0 / (1.0 + jnp.exp(-x))


def _silu(x):
    return x * _sigmoid(x)


def _log_sigmoid(x):
    return jnp.minimum(x, 0.0) - jnp.log1p(jnp.exp(-jnp.abs(x)))


def _inproj_kernel(h_ref, gmix_ref, w_ref, bf_ref, gq_ref, gk_ref, bd_ref, tri_ref,
                   q4_ref, augq_ref, k4_ref, augk_ref, vt_ref, k_ref, v_ref, logf_ref, u_ref,
                   carry_ref):
    i = pl.program_id(0)
    xn = _rms(h_ref[...], gmix_ref[...]).astype(BF16)

    def proj(c0, c1):
        return jnp.dot(xn, w_ref[:, c0:c1], preferred_element_type=F32)

    def head_norm(z, g):
        ms = jnp.dot((z * z).astype(BF16), bd_ref[...], preferred_element_type=F32)
        return z * lax.rsqrt(ms + EPS) * g

    q = head_norm(proj(0, 512), gq_ref[...])
    k = head_norm(proj(512, 1024), gk_ref[...])
    v = proj(1024, 1536)
    ua = proj(1536, 2048)
    ug = proj(2048, 2560)
    zf = proj(2560, IN_PAD)

    lane = lax.broadcasted_iota(jnp.int32, (TM, LANES), 1)
    logf = jnp.where(lane < N_HEADS, _log_sigmoid(zf + bf_ref[...]), 0.0)

    k_ref[...] = k
    v_ref[...] = v
    u_ref[...] = ua * _sigmoid(ug)
    logf_ref[...] = logf[:, :N_HEADS]

    qb = q.astype(BF16)
    kb = k.astype(BF16)
    for j in range(4):
        q4_ref[0, j] = qb[:, j * LANES:(j + 1) * LANES]
        k4_ref[0, j] = kb[:, j * LANES:(j + 1) * LANES]
    vt_ref[0, :, 0:HEAD_DIM, :] = v.T.astype(BF16).reshape(N_HEADS, HEAD_DIM, TM)
    vt_ref[0, :, HEAD_DIM:VT_ROWS, :] = jnp.ones((N_HEADS, VT_ROWS - HEAD_DIM, TM), BF16)

    hi, mid, lo = _split3(logf)
    tri = tri_ref[...]
    c = (jnp.dot(tri, hi, preferred_element_type=F32)
         + jnp.dot(tri, mid, preferred_element_type=F32)
         + jnp.dot(tri, lo, preferred_element_type=F32))
    carry = jnp.where(i % TILES_PER_SEQ == 0, 0.0, carry_ref[0:1, :])
    c = c + carry
    carry_ref[...] = jnp.broadcast_to(c[TM - 1:TM, :], (8, LANES))

    chi, cmid, clo = (p.astype(F32) for p in _split3(c))
    ones_q = jnp.where((lane >= 24) & (lane < 48), 1.0, 0.0)
    ones_k = jnp.where(lane < 24, 1.0, 0.0)
    augq = chi + pltpu.roll(cmid, 8, 1) + pltpu.roll(clo, 16, 1) + ones_q
    augk = ones_k - pltpu.roll(chi, 24, 1) - pltpu.roll(cmid, 32, 1) - pltpu.roll(clo, 40, 1)
    augq_ref[...] = augq.astype(BF16)
    augk_ref[...] = augk.astype(BF16)


def _inproj(h, gmix, w, bfp, gq, gk, bd, tri):
    row = lambda i: (i, 0)
    out_shape = (
        jax.ShapeDtypeStruct((NT, 4, TM, LANES), BF16),
        jax.ShapeDtypeStruct((N_TOK, LANES), BF16),
        jax.ShapeDtypeStruct((NT, 4, TM, LANES), BF16),
        jax.ShapeDtypeStruct((N_TOK, LANES), BF16),
        jax.ShapeDtypeStruct((NT, N_HEADS, VT_ROWS, TM), BF16),
        jax.ShapeDtypeStruct((N_TOK, ATT_WIDTH), F32),
        jax.ShapeDtypeStruct((N_TOK, ATT_WIDTH), F32),
        jax.ShapeDtypeStruct((N_TOK, N_HEADS), F32),
        jax.ShapeDtypeStruct((N_TOK, CONV_CH), F32),
    )
    out_specs = (
        pl.BlockSpec((1, 4, TM, LANES), lambda i: (i, 0, 0, 0)),
        pl.BlockSpec((TM, LANES), row),
        pl.BlockSpec((1, 4, TM, LANES), lambda i: (i, 0, 0, 0)),
        pl.BlockSpec((TM, LANES), row),
        pl.BlockSpec((1, N_HEADS, VT_ROWS, TM), lambda i: (i, 0, 0, 0)),
        pl.BlockSpec((TM, ATT_WIDTH), row),
        pl.BlockSpec((TM, ATT_WIDTH), row),
        pl.BlockSpec((TM, N_HEADS), row),
        pl.BlockSpec((TM, CONV_CH), row),
    )
    return pl.pallas_call(
        _inproj_kernel,
        grid=(NT,),
        in_specs=[pl.BlockSpec((TM, D_MODEL), row),
                  _const_spec((1, D_MODEL)), _const_spec((D_MODEL, IN_PAD)), _const_spec((1, LANES)),
                  _const_spec((1, ATT_WIDTH)), _const_spec((1, ATT_WIDTH)),
                  _const_spec((ATT_WIDTH, ATT_WIDTH)), _const_spec((TM, TM))],
        out_specs=out_specs,
        out_shape=out_shape,
        scratch_shapes=[pltpu.VMEM((8, LANES), F32)],
        compiler_params=_params(("arbitrary",)),
        name="inproj",
    )(h, gmix, w, bfp, gq, gk, bd, tri)


def _attn_kernel(q4_ref, augq_ref, k4_ref, augk_ref, vt_ref, gatt_ref, o_ref, qa_scr, m_scr, acc_scr, st_scr):
    qi = pl.program_id(1)
    lane = lax.broadcasted_iota(jnp.int32, (TM, LANES), 1)
    augq = augq_ref[...]
    zero = jnp.zeros((), BF16)
    for h in range(N_HEADS):
        qm = jnp.where((lane // HEAD_DIM) == (h % 2), q4_ref[0, h // 2], zero)
        am = jnp.where(((lane % 8) == h) & (lane < 48), augq, zero)
        qa_scr[h] = jnp.concatenate([qm, am], axis=1)
    m_scr[...] = jnp.full((N_HEADS, 1, TM), NEG, F32)
    acc_scr[...] = jnp.zeros((N_HEADS, VT_ROWS, TM), F32)

    key_pos = lax.broadcasted_iota(jnp.int32, (TM, TM), 0)
    qry_pos = lax.broadcasted_iota(jnp.int32, (TM, TM), 1)
    causal = key_pos <= qry_pos

    def scores(kb, slot, diagonal):
        aug = augk_ref[kb]
        for h in range(N_HEADS):
            ka = jnp.concatenate([k4_ref[kb, h // 2], aug], axis=1)
            st = lax.dot_general(ka, qa_scr[h], (((1,), (1,)), ((), ())),
                                 preferred_element_type=F32)
            st_scr[slot, h] = jnp.where(causal, st, NEG) if diagonal else st

    def softmax_pv(kb, slot):
        for h in range(N_HEADS):
            m = m_scr[h]
            m_new = jnp.maximum(m, jnp.max(st_scr[slot, h], axis=0, keepdims=True))
            p = jnp.exp(st_scr[slot, h] - m_new).astype(BF16)
            pv = jnp.dot(vt_ref[kb, h], p, preferred_element_type=F32)
            acc_scr[h] = jnp.exp(m - m_new) * acc_scr[h] + pv
            m_scr[h] = m_new

    @pl.when(qi == 0)
    def _():
        scores(0, 0, True)
        softmax_pv(0, 0)

    @pl.when(qi > 0)
    def _():
        scores(0, 0, False)

    def body(kk, carry):
        kb = 2 * kk
        scores(kb + 1, 1, False)
        softmax_pv(kb, 0)
        scores(kb + 2, 0, False)
        softmax_pv(kb + 1, 1)
        return carry

    lax.fori_loop(0, jnp.maximum(qi - 1, 0) // 2, body, 0)

    @pl.when(qi % 2 == 1)
    def _():
        scores(qi, 1, True)
        softmax_pv(qi - 1, 0)
        softmax_pv(qi, 1)

    @pl.when((qi % 2 == 0) & (qi > 0))
    def _():
        scores(qi - 1, 1, False)
        softmax_pv(qi - 2, 0)
        scores(qi, 0, True)
        softmax_pv(qi - 1, 1)
        softmax_pv(qi, 0)

    acc = acc_scr[...]
    ot = acc[:, 0:HEAD_DIM, :] / acc[:, HEAD_DIM:HEAD_DIM + 1, :]
    o = ot.reshape(ATT_WIDTH, TM).T
    o_ref[...] = _rms(o, gatt_ref[...]).astype(BF16)


def _attn_prompt(q4, augq, k4, augk3, vt, gatt):
    seq_blk = lambda b, qi: (b, 0, 0, 0)
    once = pl.Buffered(1)
    return pl.pallas_call(
        _attn_kernel,
        grid=(BATCH, TILES_PER_SEQ),
        in_specs=[pl.BlockSpec((1, 4, TM, LANES), lambda b, qi: (b * TILES_PER_SEQ + qi, 0, 0, 0)),
                  pl.BlockSpec((TM, LANES), lambda b, qi: (b * TILES_PER_SEQ + qi, 0)),
                  pl.BlockSpec((TILES_PER_SEQ, 4, TM, LANES), seq_blk, pipeline_mode=once),
                  pl.BlockSpec((TILES_PER_SEQ, TM, LANES), lambda b, qi: (b, 0, 0), pipeline_mode=once),
                  pl.BlockSpec((TILES_PER_SEQ, N_HEADS, VT_ROWS, TM), seq_blk, pipeline_mode=once),
                  _const_spec((1, ATT_WIDTH))],
        out_specs=pl.BlockSpec((TM, ATT_WIDTH), lambda b, qi: (b * TILES_PER_SEQ + qi, 0)),
        out_shape=jax.ShapeDtypeStruct((N_PROMPT, ATT_WIDTH), BF16),
        scratch_shapes=[pltpu.VMEM((N_HEADS, TM, 2 * LANES), BF16),
                        pltpu.VMEM((N_HEADS, 1, TM), F32),
                        pltpu.VMEM((N_HEADS, VT_ROWS, TM), F32),
                        pltpu.VMEM((2, N_HEADS, TM, TM), F32)],
        compiler_params=_params(("arbitrary", "arbitrary")),
        name="attn_prompt",
    )(q4, augq, k4, augk3, vt, gatt)


def _attn_sample_kernel(pt_ref, *refs):
    del pt_ref
    n = PAGES_PER_STEP
    kp, vp, fp = refs[:n], refs[n:2 * n], refs[2 * n:3 * n]
    (qblk_ref, lfn_ref, knew_ref, vnew_ref, upper_ref, tri_ref, gatt_ref,
     o_ref, m_scr, l_scr, acc_scr, cb_scr, carry_scr) = refs[3 * n:]
    s = pl.program_id(1)
    rows = DEC_SEQ * N_HEADS
    qblk = qblk_ref[0]
    row = lax.broadcasted_iota(jnp.int32, (rows, LANES), 0)
    lane = lax.broadcasted_iota(jnp.int32, (rows, LANES), 1)
    contract_last = (((1,), (1,)), ((), ()))

    def over_tokens(x, mat_ref):
        return sum(jnp.dot(p, mat_ref[...], preferred_element_type=F32) for p in _split3(x))

    @pl.when(s == 0)
    def _():
        cn_t = over_tokens(lfn_ref[0], tri_ref)
        cn_rows = jnp.concatenate([cn_t] * DEC_SEQ, axis=0)
        cb = jnp.sum(jnp.where(lane == row // N_HEADS, cn_rows, 0.0), axis=-1, keepdims=True)
        cb_scr[...] = jnp.broadcast_to(cb, (rows, LANES))
        carry_scr[...] = jnp.zeros((N_HEADS, LANES), F32)
        pad = jnp.zeros((LANES - DEC_SEQ, ATT_WIDTH), BF16)
        kn = jnp.concatenate([knew_ref[0].astype(BF16), pad], axis=0)
        vn = jnp.concatenate([vnew_ref[0].astype(BF16), pad], axis=0)
        st = lax.dot_general(qblk, kn, contract_last, preferred_element_type=F32)
        st = jnp.where((lane < DEC_SEQ) & (lane <= row // N_HEADS), st + cb - cn_rows, NEG)
        m = jnp.max(st, axis=-1, keepdims=True)
        p = jnp.exp(st - m)
        m_scr[...] = jnp.broadcast_to(m, (rows, LANES))
        l_scr[...] = jnp.broadcast_to(jnp.sum(p, axis=-1, keepdims=True), (rows, LANES))
        acc_scr[...] = jnp.dot(p.astype(BF16), vn, preferred_element_type=F32)

    lf = jnp.concatenate([r[0, 0] for r in fp], axis=0)
    inside = over_tokens(lf, upper_ref)
    total = jnp.sum(lf, axis=-1, keepdims=True)
    carry = carry_scr[...]
    bias = []
    for j in range(n):
        d = inside[j * N_HEADS:(j + 1) * N_HEADS] + carry
        carry = carry + total[j * N_HEADS:(j + 1) * N_HEADS]
        bias.append(jnp.concatenate([d] * DEC_SEQ, axis=0))
    carry_scr[...] = carry
    cb = cb_scr[...]
    bias = jnp.concatenate([b + cb for b in bias], axis=1)

    kt = jnp.concatenate([r[0, 0].reshape(ATT_WIDTH, PAGE_SIZE).astype(BF16) for r in kp], axis=1)
    st = jnp.dot(qblk, kt, preferred_element_type=F32) + bias
    m_old = m_scr[:, 0:1]
    m_new = jnp.maximum(m_old, jnp.max(st, axis=-1, keepdims=True))
    alpha = jnp.exp(m_old - m_new)
    p = jnp.exp(st - m_new)
    l_new = alpha * l_scr[:, 0:1] + jnp.sum(p, axis=-1, keepdims=True)
    vt = jnp.concatenate([r[0, 0].reshape(ATT_WIDTH, PAGE_SIZE).astype(BF16) for r in vp], axis=1)
    pv = lax.dot_general(p.astype(BF16), vt, contract_last, preferred_element_type=F32)
    acc = alpha * acc_scr[...] + pv
    m_scr[...] = jnp.broadcast_to(m_new, (rows, LANES))
    l_scr[...] = jnp.broadcast_to(l_new, (rows, LANES))
    acc_scr[...] = acc

    @pl.when(s == STEPS_PER_SEQ - 1)
    def _():
        o = acc / l_new
        r2 = lax.broadcasted_iota(jnp.int32, (rows, ATT_WIDTH), 0)
        c2 = lax.broadcasted_iota(jnp.int32, (rows, ATT_WIDTH), 1)
        o = jnp.where(c2 // HEAD_DIM == r2 % N_HEADS, o, 0.0)
        att = jnp.sum(o.reshape(DEC_SEQ, N_HEADS, ATT_WIDTH), axis=1)
        o_ref[...] = _rms(att, gatt_ref[...])


def _attn_sample(layer, page_table, k_t, v_t, lf_t, qblk, lfn, knew, vnew, upper, tri, gatt):
    def page(j):
        return lambda b, s, pt: pt[b, N_PAGES - 1 - (s * PAGES_PER_STEP + j)]

    def kv_spec(j):
        return pl.BlockSpec((1, 1, N_HEADS, HEAD_DIM, PAGE_SIZE),
                            lambda b, s, pt: (layer, page(j)(b, s, pt), 0, 0, 0))

    def lf_spec(j):
        return pl.BlockSpec((1, 1, N_HEADS, PAGE_SIZE), lambda b, s, pt: (layer, page(j)(b, s, pt), 0, 0))

    rows = DEC_SEQ * N_HEADS
    per_seq = lambda shape: pl.BlockSpec((1,) + shape, lambda b, s, pt: (b, 0, 0))
    const = lambda shape: pl.BlockSpec(shape, lambda b, s, pt: (0, 0))
    pages = range(PAGES_PER_STEP)
    grid_spec = pltpu.PrefetchScalarGridSpec(
        num_scalar_prefetch=1,
        grid=(DEC_BATCH, STEPS_PER_SEQ),
        in_specs=([kv_spec(j) for j in pages] + [kv_spec(j) for j in pages] + [lf_spec(j) for j in pages]
                  + [per_seq((rows, ATT_WIDTH)), per_seq((N_HEADS, LANES)),
                     per_seq((DEC_SEQ, ATT_WIDTH)), per_seq((DEC_SEQ, ATT_WIDTH)),
                     const((PAGE_SIZE, PAGE_SIZE)), const((LANES, LANES)), const((1, ATT_WIDTH))]),
        out_specs=pl.BlockSpec((DEC_SEQ, ATT_WIDTH), lambda b, s, pt: (b, 0)),
        scratch_shapes=[pltpu.VMEM((rows, LANES), F32), pltpu.VMEM((rows, LANES), F32),
                        pltpu.VMEM((rows, ATT_WIDTH), F32), pltpu.VMEM((rows, LANES), F32),
                        pltpu.VMEM((N_HEADS, LANES), F32)],
    )
    return pl.pallas_call(
        _attn_sample_kernel,
        grid_spec=grid_spec,
        out_shape=jax.ShapeDtypeStruct((N_SAMPLE, ATT_WIDTH), F32),
        compiler_params=_params(("arbitrary", "arbitrary")),
        name="attn_sample",
    )(page_table, *([k_t] * PAGES_PER_STEP), *([v_t] * PAGES_PER_STEP), *([lf_t] * PAGES_PER_STEP),
      qblk, lfn, knew, vnew, upper, tri, gatt)


def _conv_kernel(u_ref, prev_ref, state_ref, w_ref, bdw_ref, gln_ref, bln_ref, gcv_ref,
                 o_ref, xs_scr, xs3_scr, y_scr):
    t = pl.program_id(0)

    @pl.when(t < NT_PROMPT)
    def _():
        first = (t % TILES_PER_SEQ) == 0
        xs_scr[0:HALO, :] = jnp.where(first, 0.0, prev_ref[...])
        xs_scr[HALO:, :] = u_ref[...]
        acc = jnp.zeros((TM, CONV_CH), F32)
        for w in range(CONV_WIDTH):
            off = w + HALO - (CONV_WIDTH - 1)
            acc = acc + xs_scr[off:off + TM, :] * w_ref[w:w + 1, :]
        y_scr[...] = acc

    @pl.when(t == NT_PROMPT)
    def _():
        xs3_scr[:, 0:HALO, :] = state_ref[...]
        xs3_scr[:, HALO:, :] = u_ref[...].reshape(DEC_BATCH, DEC_SEQ, CONV_CH)
        acc = jnp.zeros((DEC_BATCH, DEC_SEQ, CONV_CH), F32)
        for w in range(CONV_WIDTH):
            off = w + HALO - (CONV_WIDTH - 1)
            acc = acc + xs3_scr[:, off:off + DEC_SEQ, :] * w_ref[w:w + 1, :]
        y_scr[...] = acc.reshape(TM, CONV_CH)

    y = y_scr[...] + bdw_ref[...]
    yc = y - jnp.mean(y, axis=-1, keepdims=True)
    yn = yc * lax.rsqrt(jnp.mean(yc * yc, axis=-1, keepdims=True) + EPS) * gln_ref[...] + bln_ref[...]
    o_ref[...] = _rms(_silu(yn), gcv_ref[...]).astype(BF16)


def _conv(u, state_pad, w, bdw, gln, bln, gcv):
    halo_blocks = TM // HALO
    return pl.pallas_call(
        _conv_kernel,
        grid=(NT,),
        in_specs=[pl.BlockSpec((TM, CONV_CH), lambda t: (t, 0)),
                  pl.BlockSpec((HALO, CONV_CH), lambda t: (jnp.maximum(t * halo_blocks - 1, 0), 0)),
                  _const_spec((DEC_BATCH, HALO, CONV_CH)),
                  _const_spec((HALO, CONV_CH)),
                  _const_spec((1, CONV_CH)), _const_spec((1, CONV_CH)),
                  _const_spec((1, CONV_CH)), _const_spec((1, CONV_CH))],
        out_specs=pl.BlockSpec((TM, CONV_CH), lambda t: (t, 0)),
        out_shape=jax.ShapeDtypeStruct((N_TOK, CONV_CH), BF16),
        scratch_shapes=[pltpu.VMEM((TM + HALO, CONV_CH), F32),
                        pltpu.VMEM((DEC_BATCH, HALO + DEC_SEQ, CONV_CH), F32),
                        pltpu.VMEM((TM, CONV_CH), F32)],
        compiler_params=_params(("arbitrary",)),
        name="conv",
    )(u, u, state_pad, w, bdw, gln, bln, gcv)


def _top2_gates(logits):
    lane = lax.broadcasted_iota(jnp.int32, logits.shape, 1)
    m1 = jnp.max(logits, axis=-1, keepdims=True)
    i1 = jnp.min(jnp.where(logits == m1, lane, LANES), axis=-1, keepdims=True)
    rest = jnp.where(lane == i1, NEG, logits)
    m2 = jnp.max(rest, axis=-1, keepdims=True)
    i2 = jnp.min(jnp.where(rest == m2, lane, LANES), axis=-1, keepdims=True)
    e = jnp.exp(m2 - m1)
    w1 = 1.0 / (1.0 + e)
    w2 = e / (1.0 + e)
    gates = jnp.where(lane == i1, w1, 0.0) + jnp.where(lane == i2, w2, 0.0)
    chosen = jnp.where((lane == i1) | (lane == i2), 1.0, 0.0)
    return gates, chosen


def _outproj_kernel(*refs, with_router):
    if with_router:
        (h_ref, attp_ref, atts_ref, cn_ref, wa_ref, wc_ref, gffn_ref, wr_ref,
         h1_ref, n_ref, gates_ref, chosen_ref, chosen_t_ref) = refs
    else:
        h_ref, attp_ref, atts_ref, cn_ref, wa_ref, wc_ref, gffn_ref, h1_ref, n_ref = refs
    t = pl.program_id(0)
    att = jnp.where(t == NT_PROMPT, atts_ref[...].astype(BF16), attp_ref[...])
    h1 = (h_ref[...]
          + jnp.dot(att, wa_ref[...], preferred_element_type=F32)
          + jnp.dot(cn_ref[...], wc_ref[...], preferred_element_type=F32))
    h1_ref[...] = h1
    n = _rms(h1, gffn_ref[...])
    n_ref[...] = n.astype(BF16)
    if with_router:
        logits = jnp.dot(n.astype(BF16), wr_ref[...], preferred_element_type=F32)
        lane = lax.broadcasted_iota(jnp.int32, logits.shape, 1)
        gates, chosen = _top2_gates(jnp.where(lane < N_EXPERTS, logits, NEG))
        gates_ref[...] = gates
        chosen_ref[...] = chosen
        chosen_t_ref[...] = chosen.T[:N_EXPERTS]


def _outproj(h, attp, atts, cn, wa, wc, gffn, router=None):
    row = lambda t: (t, 0)
    in_specs = [pl.BlockSpec((TM, D_MODEL), row),
                pl.BlockSpec((TM, ATT_WIDTH), lambda t: (jnp.minimum(t, NT_PROMPT - 1), 0)),
                _const_spec((N_SAMPLE, ATT_WIDTH)),
                pl.BlockSpec((TM, CONV_CH), row),
                _const_spec((ATT_WIDTH, D_MODEL)), _const_spec((CONV_CH, D_MODEL)),
                _const_spec((1, D_MODEL))]
    out_specs = [pl.BlockSpec((TM, D_MODEL), row), pl.BlockSpec((TM, D_MODEL), row)]
    out_shape = [jax.ShapeDtypeStruct((N_TOK, D_MODEL), F32), jax.ShapeDtypeStruct((N_TOK, D_MODEL), BF16)]
    args = [h, attp, atts, cn, wa, wc, gffn]
    if router is not None:
        in_specs += [_const_spec((D_MODEL, LANES))]
        out_specs += [pl.BlockSpec((TM, LANES), row), pl.BlockSpec((TM, LANES), row),
                      pl.BlockSpec((N_EXPERTS, TM), lambda t: (0, t))]
        out_shape += [jax.ShapeDtypeStruct((N_TOK, LANES), F32), jax.ShapeDtypeStruct((N_TOK, LANES), F32),
                      jax.ShapeDtypeStruct((N_EXPERTS, N_TOK), F32)]
        args.append(router)
    return pl.pallas_call(
        functools.partial(_outproj_kernel, with_router=router is not None),
        grid=(NT,),
        in_specs=in_specs,
        out_specs=tuple(out_specs),
        out_shape=tuple(out_shape),
        compiler_params=_params(("arbitrary",)),
        name="outproj",
    )(*args)


def _ple(h2, pp_ref, ps_ref, gple_ref, wproj_ref, wgate_ref):
    p = jnp.where(pl.program_id(0) == NT_PROMPT, ps_ref[0], pp_ref[...])
    gate = _sigmoid(jnp.dot(_rms(h2, gple_ref[...]).astype(BF16), wgate_ref[...],
                            preferred_element_type=F32))
    proj = jnp.dot(p.astype(BF16), wproj_ref[...], preferred_element_type=F32)
    return h2 + proj * gate


def _ple_specs(layer):
    return [pl.BlockSpec((TM, PLE_DIM), lambda t: (layer * NT_PROMPT + jnp.minimum(t, NT_PROMPT - 1), 0)),
            pl.BlockSpec((1, N_SAMPLE, PLE_DIM), lambda t: (layer, 0, 0))]


def _ffn_ple_kernel(h1_ref, n_ref, wg_ref, wu_ref, wd_ref, pp_ref, ps_ref, gple_ref, wproj_ref, wgate_ref, o_ref):
    n = n_ref[...]
    g = jnp.dot(n, wg_ref[...], preferred_element_type=F32)
    u = jnp.dot(n, wu_ref[...], preferred_element_type=F32)
    a = (_silu(g) * u).astype(BF16)
    h2 = h1_ref[...] + jnp.dot(a, wd_ref[...], preferred_element_type=F32)
    o_ref[...] = _ple(h2, pp_ref, ps_ref, gple_ref, wproj_ref, wgate_ref)


def _ffn_ple(layer, h1, n, wg, wu, wd, pp, ps, gple, wproj, wgate):
    row = lambda t: (t, 0)
    return pl.pallas_call(
        _ffn_ple_kernel,
        grid=(NT,),
        in_specs=[pl.BlockSpec((TM, D_MODEL), row), pl.BlockSpec((TM, D_MODEL), row),
                  _const_spec((D_MODEL, D_FF)), _const_spec((D_MODEL, D_FF)), _const_spec((D_FF, D_MODEL)),
                  *_ple_specs(layer), _const_spec((1, D_MODEL)),
                  _const_spec((PLE_DIM, D_MODEL)), _const_spec((D_MODEL, D_MODEL))],
        out_specs=pl.BlockSpec((TM, D_MODEL), row),
        out_shape=jax.ShapeDtypeStruct((N_TOK, D_MODEL), F32),
        compiler_params=_params(("arbitrary",)),
        name="ffn_ple",
    )(h1, n, wg, wu, wd, pp, ps, gple, wproj, wgate)


def _moe_kernel(cnt_ref, h1_ref, n_ref, gates_ref, chosen_t_ref, before_ref, wg_ref, wu_ref, wd_ref,
                o_ref, acc_scr, slot_scr, xe_scr, ye_scr):
    t = pl.program_id(0)
    e = pl.program_id(1)
    c = pl.program_id(2)
    last_c = pl.num_programs(2) - 1
    n_chunks = (cnt_ref[t, e] + MOE_CH - 1) // MOE_CH

    @pl.when((e == 0) & (c == 0))
    def _():
        acc_scr[...] = h1_ref[...]
        chosen_t = chosen_t_ref[...]
        before = jnp.dot(chosen_t.astype(BF16), before_ref[...], preferred_element_type=F32)
        slot_scr[...] = jnp.where(chosen_t > 0.5, before, -1.0)

    def selection(j):
        slot = slot_scr[pl.ds(e, 1), :]
        row = (lax.broadcasted_iota(jnp.int32, (MOE_CH, MOE_TM), 0) + j * MOE_CH).astype(F32)
        return jnp.where(slot == row, 1.0, 0.0).astype(BF16)

    @pl.when(c == 0)
    def _():
        def gather(j, carry):
            r0 = pl.multiple_of(j * MOE_CH, MOE_CH)
            xe_scr[pl.ds(r0, MOE_CH), :] = jnp.dot(
                selection(j), n_ref[...], preferred_element_type=F32).astype(BF16)
            return carry
        lax.fori_loop(0, n_chunks, gather, 0)

    def expert(j, carry):
        r0 = pl.multiple_of(j * MOE_CH, MOE_CH)
        x = xe_scr[pl.ds(r0, MOE_CH), :]
        g = jnp.dot(x, wg_ref[0, 0], preferred_element_type=F32)
        u = jnp.dot(x, wu_ref[0, 0], preferred_element_type=F32)
        y = jnp.dot((_silu(g) * u).astype(BF16), wd_ref[0, 0], preferred_element_type=F32)

        @pl.when(c == 0)
        def _():
            ye_scr[pl.ds(r0, MOE_CH), :] = y

        @pl.when(c > 0)
        def _():
            ye_scr[pl.ds(r0, MOE_CH), :] += y
        return carry

    lax.fori_loop(0, n_chunks, expert, 0)

    @pl.when(c == last_c)
    def _():
        lane = lax.broadcasted_iota(jnp.int32, (MOE_TM, LANES), 1)
        gate = jnp.sum(jnp.where(lane == e, gates_ref[...], 0.0), axis=-1, keepdims=True)

        def scatter(j, carry):
            r0 = pl.multiple_of(j * MOE_CH, MOE_CH)
            y = ye_scr[pl.ds(r0, MOE_CH), :]
            y_hi = y.astype(BF16)
            y_lo = (y - y_hi.astype(F32)).astype(BF16)
            sel = selection(j)
            back = lax.dot_general(jnp.concatenate([sel, sel], axis=0), jnp.concatenate([y_hi, y_lo], axis=0),
                                   (((0,), (0,)), ((), ())), preferred_element_type=F32)
            acc_scr[...] += gate * back
            return carry
        lax.fori_loop(0, n_chunks, scatter, 0)

    @pl.when((e == N_EXPERTS - 1) & (c == last_c))
    def _():
        o_ref[...] = acc_scr[...]


def _moe(layer, counts, h1, n, gates, chosen_t, before, wg, wu, wd):
    row = lambda t, e, c, cnt: (t, 0)
    once = pl.Buffered(1)
    grid_spec = pltpu.PrefetchScalarGridSpec(
        num_scalar_prefetch=1,
        grid=(N_TOK // MOE_TM, N_EXPERTS, D_FF_EXPERT // MOE_TF),
        in_specs=[pl.BlockSpec((MOE_TM, D_MODEL), row, pipeline_mode=once),
                  pl.BlockSpec((MOE_TM, D_MODEL), row, pipeline_mode=once),
                  pl.BlockSpec((MOE_TM, LANES), row, pipeline_mode=once),
                  pl.BlockSpec((N_EXPERTS, MOE_TM), lambda t, e, c, cnt: (0, t), pipeline_mode=once),
                  pl.BlockSpec((MOE_TM, MOE_TM), lambda t, e, c, cnt: (0, 0), pipeline_mode=once),
                  pl.BlockSpec((1, 1, D_MODEL, MOE_TF), lambda t, e, c, cnt: (layer, e, 0, c)),
                  pl.BlockSpec((1, 1, D_MODEL, MOE_TF), lambda t, e, c, cnt: (layer, e, 0, c)),
                  pl.BlockSpec((1, 1, MOE_TF, D_MODEL), lambda t, e, c, cnt: (layer, e, c, 0))],
        out_specs=pl.BlockSpec((MOE_TM, D_MODEL), row),
        scratch_shapes=[pltpu.VMEM((MOE_TM, D_MODEL), F32),
                        pltpu.VMEM((N_EXPERTS, MOE_TM), F32),
                        pltpu.VMEM((MOE_TM, D_MODEL), BF16),
                        pltpu.VMEM((MOE_TM, D_MODEL), F32)],
    )
    return pl.pallas_call(
        _moe_kernel,
        grid_spec=grid_spec,
        out_shape=jax.ShapeDtypeStruct((N_TOK, D_MODEL), F32),
        compiler_params=_params(("arbitrary", "arbitrary", "arbitrary")),
        name="moe",
    )(counts, h1, n, gates, chosen_t, before, wg, wu, wd)


def _ple_kernel(h2_ref, pp_ref, ps_ref, gple_ref, wproj_ref, wgate_ref, o_ref):
    o_ref[...] = _ple(h2_ref[...], pp_ref, ps_ref, gple_ref, wproj_ref, wgate_ref)


def _ple_call(layer, h2, pp, ps, gple, wproj, wgate):
    row = lambda t: (t, 0)
    return pl.pallas_call(
        _ple_kernel,
        grid=(NT,),
        in_specs=[pl.BlockSpec((TM, D_MODEL), row), *_ple_specs(layer),
                  _const_spec((1, D_MODEL)), _const_spec((PLE_DIM, D_MODEL)),
                  _const_spec((D_MODEL, D_MODEL))],
        out_specs=pl.BlockSpec((TM, D_MODEL), row),
        out_shape=jax.ShapeDtypeStruct((N_TOK, D_MODEL), F32),
        compiler_params=_params(("arbitrary",)),
        name="ple",
    )(h2, pp, ps, gple, wproj, wgate)


def _sample_mixing(layer, page_table, k_t, v_t, lf_t, q_tile, k_new, v_new, logf_new, gatt):
    tok = jnp.arange(PAGE_SIZE)
    upper = (tok[:, None] > tok[None, :]).astype(BF16)
    tri = (tok[:, None] <= tok[None, :]).astype(BF16)
    q_s = q_tile.transpose(1, 0, 2).reshape(DEC_BATCH, DEC_SEQ, N_HEADS, HEAD_DIM)
    eye = jnp.eye(N_HEADS, dtype=BF16)
    qblk = (q_s[:, :, :, None, :] * eye[None, None, :, :, None]).reshape(
        DEC_BATCH, DEC_SEQ * N_HEADS, ATT_WIDTH)
    lfn = jnp.pad(logf_new.reshape(DEC_BATCH, DEC_SEQ, N_HEADS).transpose(0, 2, 1),
                  ((0, 0), (0, 0), (0, LANES - DEC_SEQ)))
    return _attn_sample(layer, page_table, k_t, v_t, lf_t, qblk, lfn,
                        k_new.reshape(DEC_BATCH, DEC_SEQ, ATT_WIDTH), v_new.reshape(DEC_BATCH, DEC_SEQ, ATT_WIDTH),
                        upper, tri, gatt)


def kernel(x_prompt, x_sample, cache_k, cache_v, cache_logf, state_conv, page_table, p_prompt, p_sample, g_mix, w_in, b_forget, g_q, g_k, w_dw, b_dw, g_conv_ln, b_conv_ln, g_out_att, g_out_conv, w_out, g_ffn, w_ff_gate, w_ff_up, w_ff_down, w_router, w_ex_gate, w_ex_up, w_ex_down, g_ple, w_ple_proj, w_ple_gate):
    h = jnp.concatenate([x_prompt.reshape(N_PROMPT, D_MODEL), x_sample.reshape(N_SAMPLE, D_MODEL)], axis=0)
    k_t = jnp.transpose(cache_k, (0, 1, 3, 4, 2))
    v_t = jnp.transpose(cache_v, (0, 1, 3, 4, 2))
    lf_t = jnp.transpose(cache_logf, (0, 1, 3, 2))
    wex_gate, wex_up, wex_down = w_ex_gate.astype(BF16), w_ex_up.astype(BF16), w_ex_down.astype(BF16)
    pp = p_prompt.reshape(DEPTH * N_PROMPT, PLE_DIM)
    ps = p_sample.reshape(DEPTH, N_SAMPLE, PLE_DIM)

    idx = jnp.arange(ATT_WIDTH)
    bd = jnp.where(idx[:, None] // HEAD_DIM == idx[None, :] // HEAD_DIM, 1.0 / HEAD_DIM, 0.0).astype(BF16)
    r = jnp.arange(TM)
    tri = (r[None, :] <= r[:, None]).astype(BF16)
    rm = jnp.arange(MOE_TM)
    before = (rm[:, None] < rm[None, :]).astype(BF16)

    states = []
    for i in range(DEPTH):
        w_pad = jnp.pad(w_in[i], ((0, 0), (0, IN_PAD - IN_WIDTH))).astype(BF16)
        bfp = jnp.pad(b_forget[i], (0, LANES - N_HEADS)).reshape(1, LANES)
        gq = (jnp.tile(g_q[i], N_HEADS) * ATT_SCALE).reshape(1, ATT_WIDTH)
        gk = jnp.tile(g_k[i], N_HEADS).reshape(1, ATT_WIDTH)
        q4, augq, k4, augk, vt, k, v, logf, u = _inproj(
            h, g_mix[i].reshape(1, D_MODEL), w_pad, bfp, gq, gk, bd, tri)
        gatt = g_out_att[i].reshape(1, ATT_WIDTH)

        attp = _attn_prompt(q4, augq, k4, augk.reshape(NT, TM, LANES), vt, gatt)

        atts = _sample_mixing(i, page_table, k_t, v_t, lf_t, q4[NT_PROMPT],
                              k[N_PROMPT:], v[N_PROMPT:], logf[N_PROMPT:], gatt)

        state_pad = jnp.pad(state_conv[i], ((0, 0), (HALO - (CONV_WIDTH - 1), 0), (0, 0)))
        w_dw_pad = jnp.pad(w_dw[i], ((0, HALO - CONV_WIDTH), (0, 0)))
        cn = _conv(u, state_pad, w_dw_pad, b_dw[i].reshape(1, CONV_CH), g_conv_ln[i].reshape(1, CONV_CH),
                   b_conv_ln[i].reshape(1, CONV_CH), g_out_conv[i].reshape(1, CONV_CH))

        wo = w_out[i].astype(BF16)
        gple = g_ple[i].reshape(1, D_MODEL)
        wproj = w_ple_proj[i].astype(BF16)
        wgate = w_ple_gate[i].astype(BF16)
        gffn = g_ffn[i].reshape(1, D_MODEL)
        j = i // 2
        if i % 2 == 0:
            h1, n = _outproj(h, attp, atts, cn, wo[:ATT_WIDTH], wo[ATT_WIDTH:], gffn)
            h = _ffn_ple(i, h1, n, w_ff_gate[j].astype(BF16), w_ff_up[j].astype(BF16),
                         w_ff_down[j].astype(BF16), pp, ps, gple, wproj, wgate)
        else:
            wr = jnp.pad(w_router[j], ((0, 0), (0, LANES - N_EXPERTS))).astype(BF16)
            h1, n, gates, chosen, chosen_t = _outproj(h, attp, atts, cn, wo[:ATT_WIDTH], wo[ATT_WIDTH:], gffn,
                                                      router=wr)
            counts = jnp.sum(chosen[:, :N_EXPERTS].reshape(N_TOK // MOE_TM, MOE_TM, N_EXPERTS),
                             axis=1).astype(jnp.int32)
            h2 = _moe(j, counts, h1, n, gates, chosen_t, before, wex_gate, wex_up, wex_down)
            h = _ple_call(i, h2, pp, ps, gple, wproj, wgate)

        conv_p = u[:N_PROMPT].reshape(BATCH, SEQ, CONV_CH)[:, SEQ - (CONV_WIDTH - 1):]
        conv_s = jnp.concatenate([state_conv[i], u[N_PROMPT:].reshape(DEC_BATCH, DEC_SEQ, CONV_CH)],
                                 axis=1)[:, DEC_SEQ:]
        states.append((k, v, logf, conv_p, conv_s))

    def stack(idx, lo, hi, shape):
        return jnp.stack([s[idx][lo:hi].reshape(shape) for s in states])

    kv_p = (BATCH, SEQ, N_HEADS, HEAD_DIM)
    kv_s = (DEC_BATCH, DEC_SEQ, N_HEADS, HEAD_DIM)
    return (h[:N_PROMPT].reshape(BATCH, SEQ, D_MODEL),
            h[N_PROMPT:].reshape(DEC_BATCH, DEC_SEQ, D_MODEL),
            stack(0, 0, N_PROMPT, kv_p), stack(1, 0, N_PROMPT, kv_p),
            stack(2, 0, N_PROMPT, (BATCH, SEQ, N_HEADS)),
            jnp.stack([s[3] for s in states]),
            stack(0, N_PROMPT, N_TOK, kv_s), stack(1, N_PROMPT, N_TOK, kv_s),
            stack(2, N_PROMPT, N_TOK, (DEC_BATCH, DEC_SEQ, N_HEADS)),
            jnp.stack([s[4] for s in states]))
```

```python
import functools

import jax
import jax.numpy as jnp
from jax import lax
from jax.experimental import pallas as pl
from jax.experimental.pallas import tpu as pltpu

F32 = jnp.float32
BF16 = jnp.bfloat16

D_MODEL = 1024
BATCH = 2
SEQ = 8192
DEPTH = 4
DEC_BATCH = 32
DEC_SEQ = 8
PAST_LEN = 8192
PAGE_SIZE = 128
N_PAGES = PAST_LEN // PAGE_SIZE
N_HEADS = 8
HEAD_DIM = 64
ATT_WIDTH = N_HEADS * HEAD_DIM
CONV_CH = D_MODEL // 2
CONV_WIDTH = 31
IN_WIDTH = 3 * ATT_WIDTH + 2 * CONV_CH + N_HEADS
D_FF = 2816
N_EXPERTS = 8
D_FF_EXPERT = D_MODEL * 7 // 2
PLE_DIM = 256
EPS = 1e-6
ATT_SCALE = HEAD_DIM ** -0.5

LANES = 128
N_PROMPT = BATCH * SEQ
N_SAMPLE = DEC_BATCH * DEC_SEQ
N_TOK = N_PROMPT + N_SAMPLE
TM = 256
NT = N_TOK // TM
NT_PROMPT = N_PROMPT // TM
TILES_PER_SEQ = SEQ // TM
IN_PAD = 3 * ATT_WIDTH + 2 * CONV_CH + LANES
VT_ROWS = HEAD_DIM + 16
HALO = 32
PAGES_PER_STEP = 16
KEYS_PER_STEP = PAGES_PER_STEP * PAGE_SIZE
STEPS_PER_SEQ = N_PAGES // PAGES_PER_STEP
MOE_TM = 1280
MOE_TF = 896
MOE_CH = 128
NEG = -1e30
VMEM_LIMIT = 56 * 1024 * 1024


def _params(sem, vmem=VMEM_LIMIT):
    return pltpu.CompilerParams(dimension_semantics=sem, vmem_limit_bytes=vmem)


def _const_spec(shape):
    nd = len(shape)
    return pl.BlockSpec(shape, lambda *_: (0,) * nd, pipeline_mode=pl.Buffered(1))


def _split3(x):
    hi = x.astype(BF16)
    r = x - hi.astype(F32)
    mid = r.astype(BF16)
    lo = (r - mid.astype(F32)).astype(BF16)
    return hi, mid, lo


def _rms(x, g):
    return x * lax.rsqrt(jnp.mean(x * x, axis=-1, keepdims=True) + EPS) * g


def _sigmoid(x):
    return 1.0 / (1.0 + jnp.exp(-x))


def _silu(x):
    return x * _sigmoid(x)


def _log_sigmoid(x):
    return jnp.minimum(x, 0.0) - jnp.log1p(jnp.exp(-jnp.abs(x)))


def _inproj_kernel(h_ref, gmix_ref, w_ref, bf_ref, gq_ref, gk_ref, bd_ref, tri_ref,
                   q4_ref, augq_ref, k4_ref, augk_ref, vt_ref, k_ref, v_ref, logf_ref, u_ref,
                   carry_ref):
    i = pl.program_id(0)
    xn = _rms(h_ref[...], gmix_ref[...]).astype(BF16)

    def proj(c0, c1):
        return jnp.dot(xn, w_ref[:, c0:c1], preferred_element_type=F32)

    def head_norm(z, g):
        ms = jnp.dot((z * z).astype(BF16), bd_ref[...], preferred_element_type=F32)
        return z * lax.rsqrt(ms + EPS) * g

    q = head_norm(proj(0, 512), gq_ref[...])
    k = head_norm(proj(512, 1024), gk_ref[...])
    v = proj(1024, 1536)
    ua = proj(1536, 2048)
    ug = proj(2048, 2560)
    zf = proj(2560, IN_PAD)

    lane = lax.broadcasted_iota(jnp.int32, (TM, LANES), 1)
    logf = jnp.where(lane < N_HEADS, _log_sigmoid(zf + bf_ref[...]), 0.0)

    k_ref[...] = k
    v_ref[...] = v
    u_ref[...] = ua * _sigmoid(ug)
    logf_ref[...] = logf[:, :N_HEADS]

    qb = q.astype(BF16)
    kb = k.astype(BF16)
    for j in range(4):
        q4_ref[0, j] = qb[:, j * LANES:(j + 1) * LANES]
        k4_ref[0, j] = kb[:, j * LANES:(j + 1) * LANES]
    vt_ref[0, :, 0:HEAD_DIM, :] = v.T.astype(BF16).reshape(N_HEADS, HEAD_DIM, TM)
    vt_ref[0, :, HEAD_DIM:VT_ROWS, :] = jnp.ones((N_HEADS, VT_ROWS - HEAD_DIM, TM), BF16)

    hi, mid, lo = _split3(logf)
    tri = tri_ref[...]
    c = (jnp.dot(tri, hi, preferred_element_type=F32)
         + jnp.dot(tri, mid, preferred_element_type=F32)
         + jnp.dot(tri, lo, preferred_element_type=F32))
    carry = jnp.where(i % TILES_PER_SEQ == 0, 0.0, carry_ref[0:1, :])
    c = c + carry
    carry_ref[...] = jnp.broadcast_to(c[TM - 1:TM, :], (8, LANES))

    chi, cmid, clo = (p.astype(F32) for p in _split3(c))
    ones_q = jnp.where((lane >= 24) & (lane < 48), 1.0, 0.0)
    ones_k = jnp.where(lane < 24, 1.0, 0.0)
    augq = chi + pltpu.roll(cmid, 8, 1) + pltpu.roll(clo, 16, 1) + ones_q
    augk = ones_k - pltpu.roll(chi, 24, 1) - pltpu.roll(cmid, 32, 1) - pltpu.roll(clo, 40, 1)
    augq_ref[...] = augq.astype(BF16)
    augk_ref[...] = augk.astype(BF16)


def _inproj(h, gmix, w, bfp, gq, gk, bd, tri):
    row = lambda i: (i, 0)
    out_shape = (
        jax.ShapeDtypeStruct((NT, 4, TM, LANES), BF16),
        jax.ShapeDtypeStruct((N_TOK, LANES), BF16),
        jax.ShapeDtypeStruct((NT, 4, TM, LANES), BF16),
        jax.ShapeDtypeStruct((N_TOK, LANES), BF16),
        jax.ShapeDtypeStruct((NT, N_HEADS, VT_ROWS, TM), BF16),
        jax.ShapeDtypeStruct((N_TOK, ATT_WIDTH), F32),
        jax.ShapeDtypeStruct((N_TOK, ATT_WIDTH), F32),
        jax.ShapeDtypeStruct((N_TOK, N_HEADS), F32),
        jax.ShapeDtypeStruct((N_TOK, CONV_CH), F32),
    )
    out_specs = (
        pl.BlockSpec((1, 4, TM, LANES), lambda i: (i, 0, 0, 0)),
        pl.BlockSpec((TM, LANES), row),
        pl.BlockSpec((1, 4, TM, LANES), lambda i: (i, 0, 0, 0)),
        pl.BlockSpec((TM, LANES), row),
        pl.BlockSpec((1, N_HEADS, VT_ROWS, TM), lambda i: (i, 0, 0, 0)),
        pl.BlockSpec((TM, ATT_WIDTH), row),
        pl.BlockSpec((TM, ATT_WIDTH), row),
        pl.BlockSpec((TM, N_HEADS), row),
        pl.BlockSpec((TM, CONV_CH), row),
    )
    return pl.pallas_call(
        _inproj_kernel,
        grid=(NT,),
        in_specs=[pl.BlockSpec((TM, D_MODEL), row),
                  _const_spec((1, D_MODEL)), _const_spec((D_MODEL, IN_PAD)), _const_spec((1, LANES)),
                  _const_spec((1, ATT_WIDTH)), _const_spec((1, ATT_WIDTH)),
                  _const_spec((ATT_WIDTH, ATT_WIDTH)), _const_spec((TM, TM))],
        out_specs=out_specs,
        out_shape=out_shape,
        scratch_shapes=[pltpu.VMEM((8, LANES), F32)],
        compiler_params=_params(("arbitrary",)),
        name="inproj",
    )(h, gmix, w, bfp, gq, gk, bd, tri)


def _attn_kernel(q4_ref, augq_ref, k4_ref, augk_ref, vt_ref, gatt_ref, o_ref, qa_scr, m_scr, acc_scr, st_scr):
    qi = pl.program_id(1)
    lane = lax.broadcasted_iota(jnp.int32, (TM, LANES), 1)
    augq = augq_ref[...]
    zero = jnp.zeros((), BF16)
    for h in range(N_HEADS):
        qm = jnp.where((lane // HEAD_DIM) == (h % 2), q4_ref[0, h // 2], zero)
        am = jnp.where(((lane % 8) == h) & (lane < 48), augq, zero)
        qa_scr[h] = jnp.concatenate([qm, am], axis=1)
    m_scr[...] = jnp.full((N_HEADS, 1, TM), NEG, F32)
    acc_scr[...] = jnp.zeros((N_HEADS, VT_ROWS, TM), F32)

    key_pos = lax.broadcasted_iota(jnp.int32, (TM, TM), 0)
    qry_pos = lax.broadcasted_iota(jnp.int32, (TM, TM), 1)
    causal = key_pos <= qry_pos

    def scores(kb, slot, diagonal):
        aug = augk_ref[kb]
        for h in range(N_HEADS):
            ka = jnp.concatenate([k4_ref[kb, h // 2], aug], axis=1)
            st = lax.dot_general(ka, qa_scr[h], (((1,), (1,)), ((), ())),
                                 preferred_element_type=F32)
            st_scr[slot, h] = jnp.where(causal, st, NEG) if diagonal else st

    def softmax_pv(kb, slot):
        for h in range(N_HEADS):
            m = m_scr[h]
            m_new = jnp.maximum(m, jnp.max(st_scr[slot, h], axis=0, keepdims=True))
            p = jnp.exp(st_scr[slot, h] - m_new).astype(BF16)
            pv = jnp.dot(vt_ref[kb, h], p, preferred_element_type=F32)
            acc_scr[h] = jnp.exp(m - m_new) * acc_scr[h] + pv
            m_scr[h] = m_new

    @pl.when(qi == 0)
    def _():
        scores(0, 0, True)
        softmax_pv(0, 0)

    @pl.when(qi > 0)
    def _():
        scores(0, 0, False)

    def body(kk, carry):
        kb = 2 * kk
        scores(kb + 1, 1, False)
        softmax_pv(kb, 0)
        scores(kb + 2, 0, False)
        softmax_pv(kb + 1, 1)
        return carry

    lax.fori_loop(0, jnp.maximum(qi - 1, 0) // 2, body, 0)

    @pl.when(qi % 2 == 1)
    def _():
        scores(qi, 1, True)
        softmax_pv(qi - 1, 0)
        softmax_pv(qi, 1)

    @pl.when((qi % 2 == 0) & (qi > 0))
    def _():
        scores(qi - 1, 1, False)
        softmax_pv(qi - 2, 0)
        scores(qi, 0, True)
        softmax_pv(qi - 1, 1)
        softmax_pv(qi, 0)

    acc = acc_scr[...]
    ot = acc[:, 0:HEAD_DIM, :] / acc[:, HEAD_DIM:HEAD_DIM + 1, :]
    o = ot.reshape(ATT_WIDTH, TM).T
    o_ref[...] = _rms(o, gatt_ref[...]).astype(BF16)


def _attn_prompt(q4, augq, k4, augk3, vt, gatt):
    seq_blk = lambda b, qi: (b, 0, 0, 0)
    once = pl.Buffered(1)
    return pl.pallas_call(
        _attn_kernel,
        grid=(BATCH, TILES_PER_SEQ),
        in_specs=[pl.BlockSpec((1, 4, TM, LANES), lambda b, qi: (b * TILES_PER_SEQ + qi, 0, 0, 0)),
                  pl.BlockSpec((TM, LANES), lambda b, qi: (b * TILES_PER_SEQ + qi, 0)),
                  pl.BlockSpec((TILES_PER_SEQ, 4, TM, LANES), seq_blk, pipeline_mode=once),
                  pl.BlockSpec((TILES_PER_SEQ, TM, LANES), lambda b, qi: (b, 0, 0), pipeline_mode=once),
                  pl.BlockSpec((TILES_PER_SEQ, N_HEADS, VT_ROWS, TM), seq_blk, pipeline_mode=once),
                  _const_spec((1, ATT_WIDTH))],
        out_specs=pl.BlockSpec((TM, ATT_WIDTH), lambda b, qi: (b * TILES_PER_SEQ + qi, 0)),
        out_shape=jax.ShapeDtypeStruct((N_PROMPT, ATT_WIDTH), BF16),
        scratch_shapes=[pltpu.VMEM((N_HEADS, TM, 2 * LANES), BF16),
                        pltpu.VMEM((N_HEADS, 1, TM), F32),
                        pltpu.VMEM((N_HEADS, VT_ROWS, TM), F32),
                        pltpu.VMEM((2, N_HEADS, TM, TM), F32)],
        compiler_params=_params(("arbitrary", "arbitrary")),
        name="attn_prompt",
    )(q4, augq, k4, augk3, vt, gatt)


def _attn_sample_kernel(pt_ref, *refs):
    del pt_ref
    n = PAGES_PER_STEP
    kp, vp, fp = refs[:n], refs[n:2 * n], refs[2 * n:3 * n]
    (qblk_ref, lfn_ref, knew_ref, vnew_ref, upper_ref, tri_ref, gatt_ref,
     o_ref, m_scr, l_scr, acc_scr, cb_scr, carry_scr) = refs[3 * n:]
    s = pl.program_id(1)
    rows = DEC_SEQ * N_HEADS
    qblk = qblk_ref[0]
    row = lax.broadcasted_iota(jnp.int32, (rows, LANES), 0)
    lane = lax.broadcasted_iota(jnp.int32, (rows, LANES), 1)
    contract_last = (((1,), (1,)), ((), ()))

    def over_tokens(x, mat_ref):
        return sum(jnp.dot(p, mat_ref[...], preferred_element_type=F32) for p in _split3(x))

    @pl.when(s == 0)
    def _():
        cn_t = over_tokens(lfn_ref[0], tri_ref)
        cn_rows = jnp.concatenate([cn_t] * DEC_SEQ, axis=0)
        cb = jnp.sum(jnp.where(lane == row // N_HEADS, cn_rows, 0.0), axis=-1, keepdims=True)
        cb_scr[...] = jnp.broadcast_to(cb, (rows, LANES))
        carry_scr[...] = jnp.zeros((N_HEADS, LANES), F32)
        pad = jnp.zeros((LANES - DEC_SEQ, ATT_WIDTH), BF16)
        kn = jnp.concatenate([knew_ref[0].astype(BF16), pad], axis=0)
        vn = jnp.concatenate([vnew_ref[0].astype(BF16), pad], axis=0)
        st = lax.dot_general(qblk, kn, contract_last, preferred_element_type=F32)
        st = jnp.where((lane < DEC_SEQ) & (lane <= row // N_HEADS), st + cb - cn_rows, NEG)
        m = jnp.max(st, axis=-1, keepdims=True)
        p = jnp.exp(st - m)
        m_scr[...] = jnp.broadcast_to(m, (rows, LANES))
        l_scr[...] = jnp.broadcast_to(jnp.sum(p, axis=-1, keepdims=True), (rows, LANES))
        acc_scr[...] = jnp.dot(p.astype(BF16), vn, preferred_element_type=F32)

    lf = jnp.concatenate([r[0, 0] for r in fp], axis=0)
    inside = over_tokens(lf, upper_ref)
    total = jnp.sum(lf, axis=-1, keepdims=True)
    carry = carry_scr[...]
    bias = []
    for j in range(n):
        d = inside[j * N_HEADS:(j + 1) * N_HEADS] + carry
        carry = carry + total[j * N_HEADS:(j + 1) * N_HEADS]
        bias.append(jnp.concatenate([d] * DEC_SEQ, axis=0))
    carry_scr[...] = carry
    cb = cb_scr[...]
    bias = jnp.concatenate([b + cb for b in bias], axis=1)

    kt = jnp.concatenate([r[0, 0].reshape(ATT_WIDTH, PAGE_SIZE).astype(BF16) for r in kp], axis=1)
    st = jnp.dot(qblk, kt, preferred_element_type=F32) + bias
    m_old = m_scr[:, 0:1]
    m_new = jnp.maximum(m_old, jnp.max(st, axis=-1, keepdims=True))
    alpha = jnp.exp(m_old - m_new)
    p = jnp.exp(st - m_new)
    l_new = alpha * l_scr[:, 0:1] + jnp.sum(p, axis=-1, keepdims=True)
    vt = jnp.concatenate([r[0, 0].reshape(ATT_WIDTH, PAGE_SIZE).astype(BF16) for r in vp], axis=1)
    pv = lax.dot_general(p.astype(BF16), vt, contract_last, preferred_element_type=F32)
    acc = alpha * acc_scr[...] + pv
    m_scr[...] = jnp.broadcast_to(m_new, (rows, LANES))
    l_scr[...] = jnp.broadcast_to(l_new, (rows, LANES))
    acc_scr[...] = acc

    @pl.when(s == STEPS_PER_SEQ - 1)
    def _():
        o = acc / l_new
        r2 = lax.broadcasted_iota(jnp.int32, (rows, ATT_WIDTH), 0)
        c2 = lax.broadcasted_iota(jnp.int32, (rows, ATT_WIDTH), 1)
        o = jnp.where(c2 // HEAD_DIM == r2 % N_HEADS, o, 0.0)
        att = jnp.sum(o.reshape(DEC_SEQ, N_HEADS, ATT_WIDTH), axis=1)
        o_ref[...] = _rms(att, gatt_ref[...])


def _attn_sample(layer, page_table, k_t, v_t, lf_t, qblk, lfn, knew, vnew, upper, tri, gatt):
    def page(j):
        return lambda b, s, pt: pt[b, N_PAGES - 1 - (s * PAGES_PER_STEP + j)]

    def kv_spec(j):
        return pl.BlockSpec((1, 1, N_HEADS, HEAD_DIM, PAGE_SIZE),
                            lambda b, s, pt: (layer, page(j)(b, s, pt), 0, 0, 0))

    def lf_spec(j):
        return pl.BlockSpec((1, 1, N_HEADS, PAGE_SIZE), lambda b, s, pt: (layer, page(j)(b, s, pt), 0, 0))

    rows = DEC_SEQ * N_HEADS
    per_seq = lambda shape: pl.BlockSpec((1,) + shape, lambda b, s, pt: (b, 0, 0))
    const = lambda shape: pl.BlockSpec(shape, lambda b, s, pt: (0, 0))
    pages = range(PAGES_PER_STEP)
    grid_spec = pltpu.PrefetchScalarGridSpec(
        num_scalar_prefetch=1,
        grid=(DEC_BATCH, STEPS_PER_SEQ),
        in_specs=([kv_spec(j) for j in pages] + [kv_spec(j) for j in pages] + [lf_spec(j) for j in pages]
                  + [per_seq((rows, ATT_WIDTH)), per_seq((N_HEADS, LANES)),
                     per_seq((DEC_SEQ, ATT_WIDTH)), per_seq((DEC_SEQ, ATT_WIDTH)),
                     const((PAGE_SIZE, PAGE_SIZE)), const((LANES, LANES)), const((1, ATT_WIDTH))]),
        out_specs=pl.BlockSpec((DEC_SEQ, ATT_WIDTH), lambda b, s, pt: (b, 0)),
        scratch_shapes=[pltpu.VMEM((rows, LANES), F32), pltpu.VMEM((rows, LANES), F32),
                        pltpu.VMEM((rows, ATT_WIDTH), F32), pltpu.VMEM((rows, LANES), F32),
                        pltpu.VMEM((N_HEADS, LANES), F32)],
    )
    return pl.pallas_call(
        _attn_sample_kernel,
        grid_spec=grid_spec,
        out_shape=jax.ShapeDtypeStruct((N_SAMPLE, ATT_WIDTH), F32),
        compiler_params=_params(("arbitrary", "arbitrary")),
        name="attn_sample",
    )(page_table, *([k_t] * PAGES_PER_STEP), *([v_t] * PAGES_PER_STEP), *([lf_t] * PAGES_PER_STEP),
      qblk, lfn, knew, vnew, upper, tri, gatt)


def _conv_kernel(u_ref, prev_ref, state_ref, w_ref, bdw_ref, gln_ref, bln_ref, gcv_ref,
                 o_ref, xs_scr, xs3_scr, y_scr):
    t = pl.program_id(0)

    @pl.when(t < NT_PROMPT)
    def _():
        first = (t % TILES_PER_SEQ) == 0
        xs_scr[0:HALO, :] = jnp.where(first, 0.0, prev_ref[...])
        xs_scr[HALO:, :] = u_ref[...]
        acc = jnp.zeros((TM, CONV_CH), F32)
        for w in range(CONV_WIDTH):
            off = w + HALO - (CONV_WIDTH - 1)
            acc = acc + xs_scr[off:off + TM, :] * w_ref[w:w + 1, :]
        y_scr[...] = acc

    @pl.when(t == NT_PROMPT)
    def _():
        xs3_scr[:, 0:HALO, :] = state_ref[...]
        xs3_scr[:, HALO:, :] = u_ref[...].reshape(DEC_BATCH, DEC_SEQ, CONV_CH)
        acc = jnp.zeros((DEC_BATCH, DEC_SEQ, CONV_CH), F32)
        for w in range(CONV_WIDTH):
            off = w + HALO - (CONV_WIDTH - 1)
            acc = acc + xs3_scr[:, off:off + DEC_SEQ, :] * w_ref[w:w + 1, :]
        y_scr[...] = acc.reshape(TM, CONV_CH)

    y = y_scr[...] + bdw_ref[...]
    yc = y - jnp.mean(y, axis=-1, keepdims=True)
    yn = yc * lax.rsqrt(jnp.mean(yc * yc, axis=-1, keepdims=True) + EPS) * gln_ref[...] + bln_ref[...]
    o_ref[...] = _rms(_silu(yn), gcv_ref[...]).astype(BF16)


def _conv(u, state_pad, w, bdw, gln, bln, gcv):
    halo_blocks = TM // HALO
    return pl.pallas_call(
        _conv_kernel,
        grid=(NT,),
        in_specs=[pl.BlockSpec((TM, CONV_CH), lambda t: (t, 0)),
                  pl.BlockSpec((HALO, CONV_CH), lambda t: (jnp.maximum(t * halo_blocks - 1, 0), 0)),
                  _const_spec((DEC_BATCH, HALO, CONV_CH)),
                  _const_spec((HALO, CONV_CH)),
                  _const_spec((1, CONV_CH)), _const_spec((1, CONV_CH)),
                  _const_spec((1, CONV_CH)), _const_spec((1, CONV_CH))],
        out_specs=pl.BlockSpec((TM, CONV_CH), lambda t: (t, 0)),
        out_shape=jax.ShapeDtypeStruct((N_TOK, CONV_CH), BF16),
        scratch_shapes=[pltpu.VMEM((TM + HALO, CONV_CH), F32),
                        pltpu.VMEM((DEC_BATCH, HALO + DEC_SEQ, CONV_CH), F32),
                        pltpu.VMEM((TM, CONV_CH), F32)],
        compiler_params=_params(("arbitrary",)),
        name="conv",
    )(u, u, state_pad, w, bdw, gln, bln, gcv)


def _top2_gates(logits):
    lane = lax.broadcasted_iota(jnp.int32, logits.shape, 1)
    m1 = jnp.max(logits, axis=-1, keepdims=True)
    i1 = jnp.min(jnp.where(logits == m1, lane, LANES), axis=-1, keepdims=True)
    rest = jnp.where(lane == i1, NEG, logits)
    m2 = jnp.max(rest, axis=-1, keepdims=True)
    i2 = jnp.min(jnp.where(rest == m2, lane, LANES), axis=-1, keepdims=True)
    e = jnp.exp(m2 - m1)
    w1 = 1.0 / (1.0 + e)
    w2 = e / (1.0 + e)
    gates = jnp.where(lane == i1, w1, 0.0) + jnp.where(lane == i2, w2, 0.0)
    chosen = jnp.where((lane == i1) | (lane == i2), 1.0, 0.0)
    return gates, chosen


def _outproj_kernel(*refs, with_router):
    if with_router:
        (h_ref, attp_ref, atts_ref, cn_ref, wa_ref, wc_ref, gffn_ref, wr_ref,
         h1_ref, n_ref, gates_ref, chosen_ref, chosen_t_ref) = refs
    else:
        h_ref, attp_ref, atts_ref, cn_ref, wa_ref, wc_ref, gffn_ref, h1_ref, n_ref = refs
    t = pl.program_id(0)
    att = jnp.where(t == NT_PROMPT, atts_ref[...].astype(BF16), attp_ref[...])
    h1 = (h_ref[...]
          + jnp.dot(att, wa_ref[...], preferred_element_type=F32)
          + jnp.dot(cn_ref[...], wc_ref[...], preferred_element_type=F32))
    h1_ref[...] = h1
    n = _rms(h1, gffn_ref[...])
    n_ref[...] = n.astype(BF16)
    if with_router:
        logits = jnp.dot(n.astype(BF16), wr_ref[...], preferred_element_type=F32)
        lane = lax.broadcasted_iota(jnp.int32, logits.shape, 1)
        gates, chosen = _top2_gates(jnp.where(lane < N_EXPERTS, logits, NEG))
        gates_ref[...] = gates
        chosen_ref[...] = chosen
        chosen_t_ref[...] = chosen.T[:N_EXPERTS]


def _outproj(h, attp, atts, cn, wa, wc, gffn, router=None):
    row = lambda t: (t, 0)
    in_specs = [pl.BlockSpec((TM, D_MODEL), row),
                pl.BlockSpec((TM, ATT_WIDTH), lambda t: (jnp.minimum(t, NT_PROMPT - 1), 0)),
                _const_spec((N_SAMPLE, ATT_WIDTH)),
                pl.BlockSpec((TM, CONV_CH), row),
                _const_spec((ATT_WIDTH, D_MODEL)), _const_spec((CONV_CH, D_MODEL)),
                _const_spec((1, D_MODEL))]
    out_specs = [pl.BlockSpec((TM, D_MODEL), row), pl.BlockSpec((TM, D_MODEL), row)]
    out_shape = [jax.ShapeDtypeStruct((N_TOK, D_MODEL), F32), jax.ShapeDtypeStruct((N_TOK, D_MODEL), BF16)]
    args = [h, attp, atts, cn, wa, wc, gffn]
    if router is not None:
        in_specs += [_const_spec((D_MODEL, LANES))]
        out_specs += [pl.BlockSpec((TM, LANES), row), pl.BlockSpec((TM, LANES), row),
                      pl.BlockSpec((N_EXPERTS, TM), lambda t: (0, t))]
        out_shape += [jax.ShapeDtypeStruct((N_TOK, LANES), F32), jax.ShapeDtypeStruct((N_TOK, LANES), F32),
                      jax.ShapeDtypeStruct((N_EXPERTS, N_TOK), F32)]
        args.append(router)
    return pl.pallas_call(
        functools.partial(_outproj_kernel, with_router=router is not None),
        grid=(NT,),
        in_specs=in_specs,
        out_specs=tuple(out_specs),
        out_shape=tuple(out_shape),
        compiler_params=_params(("arbitrary",)),
        name="outproj",
    )(*args)


def _ple(h2, pp_ref, ps_ref, gple_ref, wproj_ref, wgate_ref):
    p = jnp.where(pl.program_id(0) == NT_PROMPT, ps_ref[0], pp_ref[...])
    gate = _sigmoid(jnp.dot(_rms(h2, gple_ref[...]).astype(BF16), wgate_ref[...],
                            preferred_element_type=F32))
    proj = jnp.dot(p.astype(BF16), wproj_ref[...], preferred_element_type=F32)
    return h2 + proj * gate


def _ple_specs(layer):
    return [pl.BlockSpec((TM, PLE_DIM), lambda t: (layer * NT_PROMPT + jnp.minimum(t, NT_PROMPT - 1), 0)),
            pl.BlockSpec((1, N_SAMPLE, PLE_DIM), lambda t: (layer, 0, 0))]


def _ffn_ple_kernel(h1_ref, n_ref, wg_ref, wu_ref, wd_ref, pp_ref, ps_ref, gple_ref, wproj_ref, wgate_ref, o_ref):
    n = n_ref[...]
    g = jnp.dot(n, wg_ref[...], preferred_element_type=F32)
    u = jnp.dot(n, wu_ref[...], preferred_element_type=F32)
    a = (_silu(g) * u).astype(BF16)
    h2 = h1_ref[...] + jnp.dot(a, wd_ref[...], preferred_element_type=F32)
    o_ref[...] = _ple(h2, pp_ref, ps_ref, gple_ref, wproj_ref, wgate_ref)


def _ffn_ple(layer, h1, n, wg, wu, wd, pp, ps, gple, wproj, wgate):
    row = lambda t: (t, 0)
    return pl.pallas_call(
        _ffn_ple_kernel,
        grid=(NT,),
        in_specs=[pl.BlockSpec((TM, D_MODEL), row), pl.BlockSpec((TM, D_MODEL), row),
                  _const_spec((D_MODEL, D_FF)), _const_spec((D_MODEL, D_FF)), _const_spec((D_FF, D_MODEL)),
                  *_ple_specs(layer), _const_spec((1, D_MODEL)),
                  _const_spec((PLE_DIM, D_MODEL)), _const_spec((D_MODEL, D_MODEL))],
        out_specs=pl.BlockSpec((TM, D_MODEL), row),
        out_shape=jax.ShapeDtypeStruct((N_TOK, D_MODEL), F32),
        compiler_params=_params(("arbitrary",)),
        name="ffn_ple",
    )(h1, n, wg, wu, wd, pp, ps, gple, wproj, wgate)


def _moe_kernel(cnt_ref, h1_ref, n_ref, gates_ref, chosen_t_ref, before_ref, wg_ref, wu_ref, wd_ref,
                o_ref, acc_scr, slot_scr, xe_scr, ye_scr):
    t = pl.program_id(0)
    e = pl.program_id(1)
    c = pl.program_id(2)
    last_c = pl.num_programs(2) - 1
    n_chunks = (cnt_ref[t, e] + MOE_CH - 1) // MOE_CH

    @pl.when((e == 0) & (c == 0))
    def _():
        acc_scr[...] = h1_ref[...]
        chosen_t = chosen_t_ref[...]
        before = jnp.dot(chosen_t.astype(BF16), before_ref[...], preferred_element_type=F32)
        slot_scr[...] = jnp.where(chosen_t > 0.5, before, -1.0)

    def selection(j):
        slot = slot_scr[pl.ds(e, 1), :]
        row = (lax.broadcasted_iota(jnp.int32, (MOE_CH, MOE_TM), 0) + j * MOE_CH).astype(F32)
        return jnp.where(slot == row, 1.0, 0.0).astype(BF16)

    @pl.when(c == 0)
    def _():
        def gather(j, carry):
            r0 = pl.multiple_of(j * MOE_CH, MOE_CH)
            xe_scr[pl.ds(r0, MOE_CH), :] = jnp.dot(
                selection(j), n_ref[...], preferred_element_type=F32).astype(BF16)
            return carry
        lax.fori_loop(0, n_chunks, gather, 0)

    def expert_rows(r0, size):
        x = xe_scr[pl.ds(r0, size), :]
        g = jnp.dot(x, wg_ref[0, 0], preferred_element_type=F32)
        u = jnp.dot(x, wu_ref[0, 0], preferred_element_type=F32)
        y = jnp.dot((_silu(g) * u).astype(BF16), wd_ref[0, 0], preferred_element_type=F32)

        @pl.when(c == 0)
        def _():
            ye_scr[pl.ds(r0, size), :] = y

        @pl.when(c > 0)
        def _():
            ye_scr[pl.ds(r0, size), :] += y

    def expert_pair(j, carry):
        expert_rows(pl.multiple_of(j * 2 * MOE_CH, 2 * MOE_CH), 2 * MOE_CH)
        return carry

    lax.fori_loop(0, n_chunks // 2, expert_pair, 0)

    @pl.when(n_chunks % 2 == 1)
    def _():
        expert_rows(pl.multiple_of((n_chunks - 1) * MOE_CH, MOE_CH), MOE_CH)

    @pl.when(c == last_c)
    def _():
        lane = lax.broadcasted_iota(jnp.int32, (MOE_TM, LANES), 1)
        gate = jnp.sum(jnp.where(lane == e, gates_ref[...], 0.0), axis=-1, keepdims=True)

        def scatter(j, carry):
            r0 = pl.multiple_of(j * MOE_CH, MOE_CH)
            y = ye_scr[pl.ds(r0, MOE_CH), :]
            y_hi = y.astype(BF16)
            y_lo = (y - y_hi.astype(F32)).astype(BF16)
            sel = selection(j)
            back = lax.dot_general(jnp.concatenate([sel, sel], axis=0), jnp.concatenate([y_hi, y_lo], axis=0),
                                   (((0,), (0,)), ((), ())), preferred_element_type=F32)
            acc_scr[...] += gate * back
            return carry
        lax.fori_loop(0, n_chunks, scatter, 0)

    @pl.when((e == N_EXPERTS - 1) & (c == last_c))
    def _():
        o_ref[...] = acc_scr[...]


def _moe(layer, counts, h1, n, gates, chosen_t, before, wg, wu, wd):
    row = lambda t, e, c, cnt: (t, 0)
    once = pl.Buffered(1)
    grid_spec = pltpu.PrefetchScalarGridSpec(
        num_scalar_prefetch=1,
        grid=(N_TOK // MOE_TM, N_EXPERTS, D_FF_EXPERT // MOE_TF),
        in_specs=[pl.BlockSpec((MOE_TM, D_MODEL), row, pipeline_mode=once),
                  pl.BlockSpec((MOE_TM, D_MODEL), row, pipeline_mode=once),
                  pl.BlockSpec((MOE_TM, LANES), row, pipeline_mode=once),
                  pl.BlockSpec((N_EXPERTS, MOE_TM), lambda t, e, c, cnt: (0, t), pipeline_mode=once),
                  pl.BlockSpec((MOE_TM, MOE_TM), lambda t, e, c, cnt: (0, 0), pipeline_mode=once),
                  pl.BlockSpec((1, 1, D_MODEL, MOE_TF), lambda t, e, c, cnt: (layer, e, 0, c)),
                  pl.BlockSpec((1, 1, D_MODEL, MOE_TF), lambda t, e, c, cnt: (layer, e, 0, c)),
                  pl.BlockSpec((1, 1, MOE_TF, D_MODEL), lambda t, e, c, cnt: (layer, e, c, 0))],
        out_specs=pl.BlockSpec((MOE_TM, D_MODEL), row),
        scratch_shapes=[pltpu.VMEM((MOE_TM, D_MODEL), F32),
                        pltpu.VMEM((N_EXPERTS, MOE_TM), F32),
                        pltpu.VMEM((MOE_TM, D_MODEL), BF16),
                        pltpu.VMEM((MOE_TM, D_MODEL), F32)],
    )
    return pl.pallas_call(
        _moe_kernel,
        grid_spec=grid_spec,
        out_shape=jax.ShapeDtypeStruct((N_TOK, D_MODEL), F32),
        compiler_params=_params(("arbitrary", "arbitrary", "arbitrary")),
        name="moe",
    )(counts, h1, n, gates, chosen_t, before, wg, wu, wd)


def _ple_kernel(h2_ref, pp_ref, ps_ref, gple_ref, wproj_ref, wgate_ref, o_ref):
    o_ref[...] = _ple(h2_ref[...], pp_ref, ps_ref, gple_ref, wproj_ref, wgate_ref)


def _ple_call(layer, h2, pp, ps, gple, wproj, wgate):
    row = lambda t: (t, 0)
    return pl.pallas_call(
        _ple_kernel,
        grid=(NT,),
        in_specs=[pl.BlockSpec((TM, D_MODEL), row), *_ple_specs(layer),
                  _const_spec((1, D_MODEL)), _const_spec((PLE_DIM, D_MODEL)),
                  _const_spec((D_MODEL, D_MODEL))],
        out_specs=pl.BlockSpec((TM, D_MODEL), row),
        out_shape=jax.ShapeDtypeStruct((N_TOK, D_MODEL), F32),
        compiler_params=_params(("arbitrary",)),
        name="ple",
    )(h2, pp, ps, gple, wproj, wgate)


def _sample_mixing(layer, page_table, k_t, v_t, lf_t, q_tile, k_new, v_new, logf_new, gatt):
    tok = jnp.arange(PAGE_SIZE)
    upper = (tok[:, None] > tok[None, :]).astype(BF16)
    tri = (tok[:, None] <= tok[None, :]).astype(BF16)
    q_s = q_tile.transpose(1, 0, 2).reshape(DEC_BATCH, DEC_SEQ, N_HEADS, HEAD_DIM)
    eye = jnp.eye(N_HEADS, dtype=BF16)
    qblk = (q_s[:, :, :, None, :] * eye[None, None, :, :, None]).reshape(
        DEC_BATCH, DEC_SEQ * N_HEADS, ATT_WIDTH)
    lfn = jnp.pad(logf_new.reshape(DEC_BATCH, DEC_SEQ, N_HEADS).transpose(0, 2, 1),
                  ((0, 0), (0, 0), (0, LANES - DEC_SEQ)))
    return _attn_sample(layer, page_table, k_t, v_t, lf_t, qblk, lfn,
                        k_new.reshape(DEC_BATCH, DEC_SEQ, ATT_WIDTH), v_new.reshape(DEC_BATCH, DEC_SEQ, ATT_WIDTH),
                        upper, tri, gatt)


def kernel(x_prompt, x_sample, cache_k, cache_v, cache_logf, state_conv, page_table, p_prompt, p_sample, g_mix, w_in, b_forget, g_q, g_k, w_dw, b_dw, g_conv_ln, b_conv_ln, g_out_att, g_out_conv, w_out, g_ffn, w_ff_gate, w_ff_up, w_ff_down, w_router, w_ex_gate, w_ex_up, w_ex_down, g_ple, w_ple_proj, w_ple_gate):
    h = jnp.concatenate([x_prompt.reshape(N_PROMPT, D_MODEL), x_sample.reshape(N_SAMPLE, D_MODEL)], axis=0)
    k_t = jnp.transpose(cache_k, (0, 1, 3, 4, 2))
    v_t = jnp.transpose(cache_v, (0, 1, 3, 4, 2))
    lf_t = jnp.transpose(cache_logf, (0, 1, 3, 2))
    wex_gate, wex_up, wex_down = w_ex_gate.astype(BF16), w_ex_up.astype(BF16), w_ex_down.astype(BF16)
    pp = p_prompt.reshape(DEPTH * N_PROMPT, PLE_DIM)
    ps = p_sample.reshape(DEPTH, N_SAMPLE, PLE_DIM)

    idx = jnp.arange(ATT_WIDTH)
    bd = jnp.where(idx[:, None] // HEAD_DIM == idx[None, :] // HEAD_DIM, 1.0 / HEAD_DIM, 0.0).astype(BF16)
    r = jnp.arange(TM)
    tri = (r[None, :] <= r[:, None]).astype(BF16)
    rm = jnp.arange(MOE_TM)
    before = (rm[:, None] < rm[None, :]).astype(BF16)

    states = []
    for i in range(DEPTH):
        w_pad = jnp.pad(w_in[i], ((0, 0), (0, IN_PAD - IN_WIDTH))).astype(BF16)
        bfp = jnp.pad(b_forget[i], (0, LANES - N_HEADS)).reshape(1, LANES)
        gq = (jnp.tile(g_q[i], N_HEADS) * ATT_SCALE).reshape(1, ATT_WIDTH)
        gk = jnp.tile(g_k[i], N_HEADS).reshape(1, ATT_WIDTH)
        q4, augq, k4, augk, vt, k, v, logf, u = _inproj(
            h, g_mix[i].reshape(1, D_MODEL), w_pad, bfp, gq, gk, bd, tri)
        gatt = g_out_att[i].reshape(1, ATT_WIDTH)

        attp = _attn_prompt(q4, augq, k4, augk.reshape(NT, TM, LANES), vt, gatt)

        atts = _sample_mixing(i, page_table, k_t, v_t, lf_t, q4[NT_PROMPT],
                              k[N_PROMPT:], v[N_PROMPT:], logf[N_PROMPT:], gatt)

        state_pad = jnp.pad(state_conv[i], ((0, 0), (HALO - (CONV_WIDTH - 1), 0), (0, 0)))
        w_dw_pad = jnp.pad(w_dw[i], ((0, HALO - CONV_WIDTH), (0, 0)))
        cn = _conv(u, state_pad, w_dw_pad, b_dw[i].reshape(1, CONV_CH), g_conv_ln[i].reshape(1, CONV_CH),
                   b_conv_ln[i].reshape(1, CONV_CH), g_out_conv[i].reshape(1, CONV_CH))

        wo = w_out[i].astype(BF16)
        gple = g_ple[i].reshape(1, D_MODEL)
        wproj = w_ple_proj[i].astype(BF16)
        wgate = w_ple_gate[i].astype(BF16)
        gffn = g_ffn[i].reshape(1, D_MODEL)
        j = i // 2
        if i % 2 == 0:
            h1, n = _outproj(h, attp, atts, cn, wo[:ATT_WIDTH], wo[ATT_WIDTH:], gffn)
            h = _ffn_ple(i, h1, n, w_ff_gate[j].astype(BF16), w_ff_up[j].astype(BF16),
                         w_ff_down[j].astype(BF16), pp, ps, gple, wproj, wgate)
        else:
            wr = jnp.pad(w_router[j], ((0, 0), (0, LANES - N_EXPERTS))).astype(BF16)
            h1, n, gates, chosen, chosen_t = _outproj(h, attp, atts, cn, wo[:ATT_WIDTH], wo[ATT_WIDTH:], gffn,
                                                      router=wr)
            counts = jnp.sum(chosen[:, :N_EXPERTS].reshape(N_TOK // MOE_TM, MOE_TM, N_EXPERTS),
                             axis=1).astype(jnp.int32)
            h2 = _moe(j, counts, h1, n, gates, chosen_t, before, wex_gate, wex_up, wex_down)
            h = _ple_call(i, h2, pp, ps, gple, wproj, wgate)

        conv_p = u[:N_PROMPT].reshape(BATCH, SEQ, CONV_CH)[:, SEQ - (CONV_WIDTH - 1):]
        conv_s = jnp.concatenate([state_conv[i], u[N_PROMPT:].reshape(DEC_BATCH, DEC_SEQ, CONV_CH)],
                                 axis=1)[:, DEC_SEQ:]
        states.append((k, v, logf, conv_p, conv_s))

    def stack(idx, lo, hi, shape):
        return jnp.stack([s[idx][lo:hi].reshape(shape) for s in states])

    kv_p = (BATCH, SEQ, N_HEADS, HEAD_DIM)
    kv_s = (DEC_BATCH, DEC_SEQ, N_HEADS, HEAD_DIM)
    return (h[:N_PROMPT].reshape(BATCH, SEQ, D_MODEL),
            h[N_PROMPT:].reshape(DEC_BATCH, DEC_SEQ, D_MODEL),
            stack(0, 0, N_PROMPT, kv_p), stack(1, 0, N_PROMPT, kv_p),
            stack(2, 0, N_PROMPT, (BATCH, SEQ, N_HEADS)),
            jnp.stack([s[3] for s in states]),
            stack(0, N_PROMPT, N_TOK, kv_s), stack(1, N_PROMPT, N_TOK, kv_s),
            stack(2, N_PROMPT, N_TOK, (DEC_BATCH, DEC_SEQ, N_HEADS)),
            jnp.stack([s[4] for s in states]))
```
